```python
import math
import jax, jax.numpy as jnp
from jax import lax
import numpy as np

D_MODEL = 1024
BATCH = 8
SEQ = 4096
DEPTH = 1
DEC_BATCH = 2
DEC_SEQ = 8192
PAST_LEN = 128

MIX_WIDTH = D_MODEL
ATT_WIDTH = MIX_WIDTH // 2
HG_WIDTH = MIX_WIDTH - ATT_WIDTH
HEAD_DIM = 64
ATT_HEADS = ATT_WIDTH // HEAD_DIM
HG_HEAD_DIM = 64
HG_HEADS = HG_WIDTH // HG_HEAD_DIM
DILATED_PATTERNS = ((128, 1), (512, 4), (2048, 16))
KEYS_PER_SIDE = 64
ATT_QBLOCK = KEYS_PER_SIDE
HG_CHUNK = 64
PEER_HEADS = 8
PEER_N_KEYS = 128
PEER_N_EXPERTS = PEER_N_KEYS * PEER_N_KEYS
PEER_TOPK = 16
PEER_KEY_DIM = 256
PEER_HALF = PEER_KEY_DIM // 2
PEER_TOKEN_BLOCK = 128
NORM_EPS = 1e-6
N_IN_COLS = 3 * ATT_WIDTH + 5 * HG_WIDTH

kernel_name = "hymba_dilated_hgrn2_peer_encoder"


def rmsnorm(x, g):
    xf = x.astype(jnp.float32)
    y = xf * lax.rsqrt(jnp.mean(xf * xf, axis=-1, keepdims=True) + NORM_EPS)
    return (y * g.astype(jnp.float32)).astype(x.dtype)


def alibi_slopes(n_heads):
    return 2.0 ** (-8.0 * jnp.arange(1, n_heads + 1, dtype=jnp.float32) / n_heads)


def dilated_window_attention(q, k, v, dil, slopes):
    B, S, H, Dh = q.shape
    QB = ATT_QBLOCK
    L = S // dil
    nb = -(-L // QB)
    Lp = nb * QB

    def to_class(t):
        return t.reshape(B, L, dil, H, Dh).transpose(0, 2, 1, 3, 4)

    qc = jnp.pad(to_class(q), ((0, 0), (0, 0), (0, Lp - L), (0, 0), (0, 0))).reshape(B, dil, nb, QB, H, Dh)

    def key_blocks(t):
        tp = jnp.pad(to_class(t), ((0, 0), (0, 0), (QB, Lp - L + QB), (0, 0), (0, 0)))
        tp = tp.reshape(B, dil, nb + 2, QB, H, Dh)
        return jnp.concatenate([tp[:, :, :-2], tp[:, :, 1:-1], tp[:, :, 2:]], axis=3)

    kb = key_blocks(k)
    vb = key_blocks(v)
    s = jnp.einsum('brnqhd,brnkhd->brnhqk', qc, kb).astype(jnp.float32) * (HEAD_DIM ** -0.5)
    qi = jnp.arange(QB)
    kj = jnp.arange(3 * QB)
    rel = qi[:, None] - kj[None, :] + QB
    k_idx = jnp.arange(nb)[:, None] * QB - QB + kj[None, :]
    valid = (jnp.abs(rel) <= KEYS_PER_SIDE)[None] & ((k_idx >= 0) & (k_idx < L))[:, None, :]
    bias = -slopes[:, None, None] * (dil * jnp.abs(rel)).astype(jnp.float32)[None]
    s = jnp.where(valid[None, None, :, None], s + bias[None, None, None], -jnp.inf)
    m = jnp.max(s, axis=-1, keepdims=True)
    p = jnp.exp(s - m)
    l = jnp.sum(p, axis=-1, keepdims=True)
    o = jnp.einsum('brnhqk,brnkhd->brnqhd', (p / l).astype(v.dtype), vb)
    lse = (m + jnp.log(l))[..., 0]
    o = o.reshape(B, dil, Lp, H, Dh)[:, :, :L].transpose(0, 2, 1, 3, 4).reshape(B, S, H, Dh)
    lse = lse.transpose(0, 1, 2, 4, 3).reshape(B, dil, Lp, H)[:, :, :L].transpose(0, 2, 1, 3).reshape(B, S, H)
    return o, lse


def dilated_attention(q, k, v, slopes):
    outs, lses = [], []
    for window, dil in DILATED_PATTERNS:
        o, lse = dilated_window_attention(q, k, v, dil, slopes)
        outs.append(o)
        lses.append(lse)
    w = jax.nn.softmax(jnp.stack(lses, axis=0), axis=0)
    o = jnp.sum(w[..., None] * jnp.stack(outs, axis=0).astype(jnp.float32), axis=0)
    return o.astype(q.dtype)


def hgrn2_scan(q, k, v, logf):
    B, S, H, Dk = q.shape
    Dv = v.shape[-1]
    C = HG_CHUNK
    nc = S // C

    def chunks(t):
        return t.astype(jnp.float32).reshape(B, nc, C, H, t.shape[-1]).transpose(1, 0, 3, 2, 4)

    causal = jnp.tril(jnp.ones((C, C), dtype=bool))

    def body(state, xs):
        qc, kc, vc, gc = xs
        cum = jnp.cumsum(gc, axis=2)
        inter = jnp.einsum('bhtd,bhde->bhte', qc * jnp.exp(cum), state)
        diff = cum[:, :, :, None, :] - cum[:, :, None, :, :]
        decay = jnp.exp(jnp.where(causal[:, :, None], diff, -jnp.inf))
        scores = jnp.sum(qc[:, :, :, None, :] * kc[:, :, None, :, :] * decay, axis=-1)
        intra = jnp.einsum('bhts,bhse->bhte', scores, vc)
        last = cum[:, :, -1:]
        new_state = jnp.exp(last[:, :, 0])[..., None] * state + jnp.einsum(
            'bhsd,bhse->bhde', kc * jnp.exp(last - cum), vc)
        return new_state, inter + intra

    state0 = jnp.zeros((B, H, Dk, Dv), jnp.float32)
    _, o = lax.scan(body, state0, (chunks(q), chunks(k), chunks(v), chunks(logf)))
    return o.transpose(1, 0, 3, 2, 4).reshape(B, S, H, Dv)


def peer_layer(xn, wq, keys, u_tab, v_tab):
    B, S, D = xn.shape
    xb = xn.reshape((B * S) // PEER_TOKEN_BLOCK, PEER_TOKEN_BLOCK, D)

    def body(xt):
        q = (xt @ wq).reshape(xt.shape[0], PEER_HEADS, 2, PEER_HALF)
        s = jnp.einsum('thpd,hpnd->thpn', q, keys).astype(jnp.float32)
        s_top, i_top = lax.top_k(s, PEER_TOPK)
        cand = s_top[:, :, 0, :, None] + s_top[:, :, 1, None, :]
        cand_idx = i_top[:, :, 0, :, None] * PEER_N_KEYS + i_top[:, :, 1, None, :]
        cand = cand.reshape(xt.shape[0], PEER_HEADS, PEER_TOPK * PEER_TOPK)
        cand_idx = cand_idx.reshape(xt.shape[0], PEER_HEADS, PEER_TOPK * PEER_TOPK)
        vals, pos = lax.top_k(cand, PEER_TOPK)
        idx = jnp.take_along_axis(cand_idx, pos, axis=-1)
        g = jax.nn.softmax(vals, axis=-1)
        u = u_tab[idx]
        h = jax.nn.gelu(jnp.einsum('td,thkd->thk', xt, u).astype(jnp.float32))
        out = jnp.einsum('thk,thkd->td', (g * h).astype(xt.dtype), v_tab[idx])
        return out

    return lax.map(body, xb).reshape(B, S, D)


def trunk(x, norm1, w_in, q_norm, k_norm, lower_bounds, hg_norm, w_out, norm2,
          peer_wq, peer_keys, peer_u, peer_v):
    B, S, D = x.shape
    slopes = alibi_slopes(ATT_HEADS)
    lbs = jnp.cumsum(jax.nn.softmax(lower_bounds.astype(jnp.float32), axis=1), axis=1)
    for l in range(DEPTH):
        xn = rmsnorm(x, norm1[l])
        proj = xn @ w_in[l]
        qa, ka, va, qh, zf, zb, ih, gh = jnp.split(
            proj, [ATT_WIDTH, 2 * ATT_WIDTH, 3 * ATT_WIDTH,
                   3 * ATT_WIDTH + HG_WIDTH, 3 * ATT_WIDTH + 2 * HG_WIDTH,
                   3 * ATT_WIDTH + 3 * HG_WIDTH, 3 * ATT_WIDTH + 4 * HG_WIDTH], axis=-1)
        qa = rmsnorm(qa.reshape(B, S, ATT_HEADS, HEAD_DIM), q_norm[l])
        ka = rmsnorm(ka.reshape(B, S, ATT_HEADS, HEAD_DIM), k_norm[l])
        va = va.reshape(B, S, ATT_HEADS, HEAD_DIM)
        att = dilated_attention(qa, ka, va, slopes).reshape(B, S, ATT_WIDTH)
        def gate(z, lb):
            f = lb + (1.0 - lb) * jax.nn.sigmoid(z.astype(jnp.float32))
            return (jnp.log(f).reshape(B, S, HG_HEADS, HG_HEAD_DIM),
                    (1.0 - f).reshape(B, S, HG_HEADS, HG_HEAD_DIM))
        logf_f, k_f = gate(zf, lbs[0, l])
        logf_b, k_b = gate(zb, lbs[1, l])
        qh4 = qh.reshape(B, S, HG_HEADS, HG_HEAD_DIM)
        ih4 = ih.reshape(B, S, HG_HEADS, HG_HEAD_DIM)
        o_f = hgrn2_scan(qh4, k_f, ih4, logf_f)
        o_b = jnp.flip(hgrn2_scan(jnp.flip(qh4, 1), jnp.flip(k_b, 1), jnp.flip(ih4, 1),
                                  jnp.flip(logf_b, 1)), 1)
        hg = rmsnorm(o_f + o_b, hg_norm[l]).astype(x.dtype)
        hg = (hg * jax.nn.silu(gh.reshape(B, S, HG_HEADS, HG_HEAD_DIM))).reshape(B, S, HG_WIDTH)
        x = x + jnp.concatenate([att, hg], axis=-1) @ w_out[l]
        x = x + peer_layer(rmsnorm(x, norm2[l]), peer_wq[l], peer_keys[l], peer_u[l], peer_v[l])
    return x


def setup_inputs(seed: int = 0) -> dict:
    key = jax.random.key(seed)
    ks = jax.random.split(key, 16)
    f32 = jnp.float32
    nrm = lambda k, shape, scale: jax.random.normal(k, shape, f32) * scale
    gain = lambda k, shape: 1.0 + 0.02 * jax.random.normal(k, shape, f32)
    return {
        "x_prompt": nrm(ks[0], (BATCH, SEQ, D_MODEL), 1.0),
        "x_sample": nrm(ks[1], (DEC_BATCH, DEC_SEQ, D_MODEL), 1.0),
        "norm1": gain(ks[2], (DEPTH, D_MODEL)),
        "w_in": nrm(ks[3], (DEPTH, D_MODEL, N_IN_COLS), D_MODEL ** -0.5),
        "q_norm": gain(ks[4], (DEPTH, HEAD_DIM)),
        "k_norm": gain(ks[5], (DEPTH, HEAD_DIM)),
        "lower_bounds": nrm(ks[6], (2, DEPTH + 1, HG_WIDTH), 0.1),
        "hg_norm": gain(ks[7], (DEPTH, HG_HEAD_DIM)),
        "w_out": nrm(ks[8], (DEPTH, MIX_WIDTH, D_MODEL), MIX_WIDTH ** -0.5),
        "norm2": gain(ks[9], (DEPTH, D_MODEL)),
        "peer_wq": nrm(ks[10], (DEPTH, D_MODEL, PEER_HEADS * PEER_KEY_DIM), D_MODEL ** -0.5),
        "peer_keys": nrm(ks[11], (DEPTH, PEER_HEADS, 2, PEER_N_KEYS, PEER_HALF), PEER_HALF ** -0.5),
        "peer_u": nrm(ks[12], (DEPTH, PEER_N_EXPERTS, D_MODEL), D_MODEL ** -0.5),
        "peer_v": nrm(ks[13], (DEPTH, PEER_N_EXPERTS, D_MODEL), PEER_HEADS ** -0.5),
    }


def reference(x_prompt, x_sample, norm1, w_in, q_norm, k_norm, lower_bounds, hg_norm,
              w_out, norm2, peer_wq, peer_keys, peer_u, peer_v):
    y_prompt = trunk(x_prompt, norm1, w_in, q_norm, k_norm, lower_bounds, hg_norm, w_out,
                     norm2, peer_wq, peer_keys, peer_u, peer_v)
    y_sample = trunk(x_sample, norm1, w_in, q_norm, k_norm, lower_bounds, hg_norm, w_out,
                     norm2, peer_wq, peer_keys, peer_u, peer_v)
    return (y_prompt, y_sample)
```

```python
import functools

import jax
import jax.numpy as jnp
from jax import lax
from jax.experimental import pallas as pl
from jax.experimental.pallas import tpu as pltpu

f32 = jnp.float32
bf16 = jnp.bfloat16
i32 = jnp.int32

D_MODEL = 1024
ATT_WIDTH = 512
HG_WIDTH = 512
HEAD_DIM = 64
N_IN_COLS = 3 * ATT_WIDTH + 5 * HG_WIDTH
DILATIONS = (1, 4, 16)
KEYS_PER_SIDE = 64
HG_CHUNK = 64
PEER_HEADS = 8
PEER_N_KEYS = 128
PEER_TOPK = 16
PEER_HALF = 128
NORM_EPS = 1e-6

LANES = 128
SUBLANES = 8
VMEM_LIMIT = 56 * 1024 * 1024
NEG_INF = float("-inf")


def _cparams(sem=None):
    return pltpu.CompilerParams(dimension_semantics=sem, vmem_limit_bytes=VMEM_LIMIT)


IN_TM = 512


def _inproj_kernel(x_ref, g_ref, w_ref, o_ref):
    x = x_ref[...]
    ms = jnp.mean(x * x, axis=-1, keepdims=True)
    xn = ((x * lax.rsqrt(ms + NORM_EPS)) * g_ref[...]).astype(bf16)
    tn = 1024
    for c in range(N_IN_COLS // tn):
        o_ref[:, c * tn:(c + 1) * tn] = jnp.dot(xn, w_ref[:, c * tn:(c + 1) * tn],
                                                preferred_element_type=f32)


def in_proj(x2d, gain, w_bf16):
    n = x2d.shape[0]
    return pl.pallas_call(
        _inproj_kernel,
        grid=(n // IN_TM,),
        in_specs=[pl.BlockSpec((IN_TM, D_MODEL), lambda i: (i, 0)),
                  pl.BlockSpec((1, D_MODEL), lambda i: (0, 0)),
                  pl.BlockSpec((D_MODEL, N_IN_COLS), lambda i: (0, 0))],
        out_specs=pl.BlockSpec((IN_TM, N_IN_COLS), lambda i: (i, 0)),
        out_shape=jax.ShapeDtypeStruct((n, N_IN_COLS), f32),
        compiler_params=_cparams(("arbitrary",)),
        name="in_proj",
    )(x2d, gain, w_bf16)


def _pair_rmsnorm(x, gain, head0):
    x2 = x * x
    s0 = jnp.sum(jnp.where(head0, x2, 0.0), axis=-1, keepdims=True)
    s1 = jnp.sum(jnp.where(head0, 0.0, x2), axis=-1, keepdims=True)
    ms = jnp.where(head0, s0, s1) * (1.0 / HEAD_DIM)
    return (x * lax.rsqrt(ms + NORM_EPS)) * gain


ATT_QB = 128
ATT_KW = ATT_QB + 2 * KEYS_PER_SIDE
ATT_CP = 256


def _attn_kernel(slopes_ref, q_ref, k_ref, v_ref, qg_ref, kg_ref, o_ref,
                 qc, kc, vc, oacc, macc, lacc, *, seq):
    hp = pl.program_id(1)
    lane = lax.broadcasted_iota(i32, (1, LANES), 1)
    head0 = lane < HEAD_DIM
    qg = qg_ref[...]
    kg = kg_ref[...]
    zpad = jnp.zeros((KEYS_PER_SIDE, LANES), bf16)
    for ref in (qc, kc, vc):
        ref[0:KEYS_PER_SIDE, :] = zpad
        ref[seq + KEYS_PER_SIDE:seq + 2 * KEYS_PER_SIDE, :] = zpad
    slope = (slopes_ref[2 * hp], slopes_ref[2 * hp + 1])

    iq = lax.broadcasted_iota(i32, (ATT_QB, ATT_KW), 0)
    jj = lax.broadcasted_iota(i32, (ATT_QB, ATT_KW), 1)
    rel = jnp.abs(iq + KEYS_PER_SIDE - jj)
    band = rel <= KEYS_PER_SIDE

    for pat, dil in enumerate(DILATIONS):
        cls_len = seq // dil
        shift = cls_len.bit_length() - 1
        cp = min(ATT_CP, cls_len)
        n_cp = cls_len // cp

        for r in range(dil):
            def copy_body(c, carry, r=r, cp=cp):
                if dil == 1:
                    src = pl.ds(pl.multiple_of(c * cp, cp), cp)
                else:
                    src = pl.ds(r + dil * cp * c, cp, stride=dil)
                dst = pl.ds(pl.multiple_of(KEYS_PER_SIDE + r * cls_len + c * cp, KEYS_PER_SIDE), cp)
                qn = _pair_rmsnorm(q_ref[src, :], qg, head0) * (HEAD_DIM ** -0.5)
                qc[dst, :] = qn.astype(bf16)
                kc[dst, :] = _pair_rmsnorm(k_ref[src, :], kg, head0).astype(bf16)
                vc[dst, :] = v_ref[src, :].astype(bf16)
                return carry
            lax.fori_loop(0, n_cp, copy_body, 0)

        relf = (dil * rel).astype(f32)
        bias = [jnp.where(band, -slope[a] * relf, NEG_INF) for a in range(2)]

        def block_body(i, carry, pat=pat, dil=dil, cls_len=cls_len, shift=shift, bias=bias):
            r0 = pl.multiple_of(i * ATT_QB, ATT_QB)
            cls = lax.shift_right_logical(r0, shift)
            lo = cls * cls_len - (r0 - KEYS_PER_SIDE)
            valid = (jj >= lo) & (jj < lo + cls_len)
            qb = qc[pl.ds(pl.multiple_of(r0 + KEYS_PER_SIDE, KEYS_PER_SIDE), ATT_QB), :]
            kw = kc[pl.ds(r0, ATT_KW), :]
            vw = vc[pl.ds(r0, ATT_KW), :]
            res = []
            for a in range(2):
                hm = head0 if a == 0 else jnp.logical_not(head0)
                qa = jnp.where(hm, qb, jnp.zeros_like(qb))
                s = lax.dot_general(qa, kw, (((1,), (1,)), ((), ())), preferred_element_type=f32)
                s = jnp.where(valid, s + bias[a], NEG_INF)
                m = jnp.max(s, axis=-1, keepdims=True)
                p = jnp.exp(s - m)
                l = jnp.sum(p, axis=-1, keepdims=True)
                o = jnp.dot(p.astype(bf16), vw, preferred_element_type=f32)
                res.append((o, m, l))
            o = jnp.where(head0, res[0][0], res[1][0])
            m = jnp.where(head0, res[0][1], res[1][1])
            l = jnp.where(head0, res[0][2], res[1][2])
            if dil == 1:
                tok = pl.ds(r0, ATT_QB)
            else:
                tok = pl.ds((r0 - cls * cls_len) * dil + cls, ATT_QB, stride=dil)
            if pat == 0:
                oacc[tok, :] = o
                macc[tok, :] = m
                lacc[tok, :] = l
            else:
                m_old = macc[tok, :]
                m_new = jnp.maximum(m_old, m)
                a_old = jnp.exp(m_old - m_new)
                a_new = jnp.exp(m - m_new)
                oacc[tok, :] = a_old * oacc[tok, :] + a_new * o
                lacc[tok, :] = a_old * lacc[tok, :] + a_new * l
                macc[tok, :] = m_new
            return carry
        lax.fori_loop(0, seq // ATT_QB, block_body, 0)

    def fin_body(c, carry):
        rows = pl.ds(pl.multiple_of(c * ATT_CP, ATT_CP), ATT_CP)
        o_ref[rows, :] = (oacc[rows, :] / lacc[rows, :]).astype(bf16)
        return carry
    lax.fori_loop(0, seq // ATT_CP, fin_body, 0)


def attention(proj3, slopes, qg2, kg2):
    b, seq, _ = proj3.shape
    n_hp = ATT_WIDTH // LANES

    def col(off):
        return pl.BlockSpec((None, seq, LANES), lambda bi, h, off=off: (bi, 0, off + h),
                            pipeline_mode=pl.Buffered(1))
    gspec = pl.BlockSpec((1, LANES), lambda bi, h: (0, 0))
    return pl.pallas_call(
        functools.partial(_attn_kernel, seq=seq),
        grid=(b, n_hp),
        in_specs=[pl.BlockSpec(memory_space=pltpu.SMEM),
                  col(0), col(n_hp), col(2 * n_hp), gspec, gspec],
        out_specs=pl.BlockSpec((None, seq, LANES), lambda bi, h: (bi, 0, h)),
        out_shape=jax.ShapeDtypeStruct((b, seq, ATT_WIDTH), bf16),
        scratch_shapes=[pltpu.VMEM((seq + 2 * KEYS_PER_SIDE, LANES), bf16)] * 3
                       + [pltpu.VMEM((seq, LANES), f32)] * 3,
        compiler_params=_cparams(("arbitrary", "arbitrary")),
        name="dilated_attention",
    )(slopes, proj3, proj3, proj3, qg2, kg2)


HG_LEVELS = 6


def _shift_rows(x, j, up):
    n = x.shape[0]
    return pltpu.roll(x, (n - j) % n if up else j, axis=0)


def _hg_chunk(q, k, v, lf, st, rev, head0_rows, blockdiag):
    c = HG_CHUNK
    row = lax.broadcasted_iota(i32, (c, LANES), 0)
    cum = lf
    for lv in range(HG_LEVELS):
        j = 1 << lv
        sh = _shift_rows(cum, j, up=rev)
        ok = (row < c - j) if rev else (row >= j)
        cum = cum + jnp.where(ok, sh, 0.0)

    trow = lax.broadcasted_iota(i32, (2 * c, c), 0) & (c - 1)
    scol = lax.broadcasted_iota(i32, (2 * c, c), 1)

    def pair_scores(a_mat, b_mat, lv):
        a2 = jnp.concatenate([jnp.where(head0_rows, a_mat, 0.0), jnp.where(head0_rows, 0.0, a_mat)], axis=0)
        sc = lax.dot_general(a2.astype(bf16), b_mat.astype(bf16), (((1,), (1,)), ((), ())),
                             preferred_element_type=f32)
        same = lax.shift_right_logical(trow, lv) == lax.shift_right_logical(scol, lv)
        return jnp.where(same, sc, 0.0)

    scores = pair_scores(q, k, 0)
    edge = cum
    for lv in range(1, HG_LEVELS + 1):
        h = 1 << (lv - 1)
        if h > 1:
            hh = h >> 1
            sh = _shift_rows(edge, hh, up=not rev)
            take = ((row & hh) == 0) != rev
            edge = jnp.where(take, sh, edge)
        far = ((row & h) != 0) != rev
        edge_prev = _shift_rows(edge, h, up=rev)
        a_mat = q * jnp.exp(jnp.where(far, cum - edge_prev, NEG_INF))
        b_mat = k * jnp.exp(jnp.where(far, NEG_INF, edge - cum))
        scores = scores + pair_scores(a_mat, b_mat, lv)

    hh = c >> 1
    sh = _shift_rows(edge, hh, up=not rev)
    total = jnp.where(((row & hh) == 0) != rev, sh, edge)

    vb = v.astype(bf16)
    scb = scores.astype(bf16)
    intra0 = jnp.dot(scb[0:c], vb, preferred_element_type=f32)
    intra1 = jnp.dot(scb[c:2 * c], vb, preferred_element_type=f32)
    intra = jnp.where(head0_rows, intra0, intra1)
    qe = (q * jnp.exp(cum)).astype(bf16)
    inter = lax.dot_general(qe, st.astype(bf16), (((1,), (1,)), ((), ())), preferred_element_type=f32)
    kd = (k * jnp.exp(total - cum)).astype(bf16)
    upd = lax.dot_general(vb, kd, (((0,), (0,)), ((), ())), preferred_element_type=f32)
    st_new = st * jnp.exp(total[0:1, :]) + jnp.where(blockdiag, upd, 0.0)
    return inter + intra, st_new


def _hgrn_kernel(lb_ref, q_ref, zf_ref, zb_ref, i_ref, g_ref, hn_ref, o_ref, of_scr, *, seq):
    c = HG_CHUNK
    lane = lax.broadcasted_iota(i32, (1, LANES), 1)
    head0 = lane < HEAD_DIM
    head0_rows = lax.broadcasted_iota(i32, (c, LANES), 1) < HEAD_DIM
    r2 = lax.broadcasted_iota(i32, (LANES, LANES), 0) < HEAD_DIM
    c2 = lax.broadcasted_iota(i32, (LANES, LANES), 1) < HEAD_DIM
    blockdiag = r2 == c2

    lbs = []
    for d in range(2):
        a0 = lb_ref[d, 0:1, :]
        a1 = lb_ref[d, 1:2, :]
        mx = jnp.maximum(a0, a1)
        e0 = jnp.exp(a0 - mx)
        e1 = jnp.exp(a1 - mx)
        lbs.append(e0 / (e0 + e1))

    n_chunks = seq // c

    def gates(z, lb):
        f = lb + (1.0 - lb) * jax.nn.sigmoid(z)
        return jnp.log(f), 1.0 - f

    def fwd_body(ci, st):
        rows = pl.ds(pl.multiple_of(ci * c, c), c)
        lf, k = gates(zf_ref[rows, :], lbs[0])
        o, st = _hg_chunk(q_ref[rows, :], k, i_ref[rows, :], lf, st, False, head0_rows, blockdiag)
        of_scr[rows, :] = o
        return st
    lax.fori_loop(0, n_chunks, fwd_body, jnp.zeros((LANES, LANES), f32))

    hn = hn_ref[...]

    def bwd_body(cj, st):
        ci = n_chunks - 1 - cj
        rows = pl.ds(pl.multiple_of(ci * c, c), c)
        lf, k = gates(zb_ref[rows, :], lbs[1])
        o, st = _hg_chunk(q_ref[rows, :], k, i_ref[rows, :], lf, st, True, head0_rows, blockdiag)
        tot = of_scr[rows, :] + o
        y = _pair_rmsnorm(tot, hn, head0)
        o_ref[rows, :] = (y * jax.nn.silu(g_ref[rows, :])).astype(bf16)
        return st
    lax.fori_loop(0, n_chunks, bwd_body, jnp.zeros((LANES, LANES), f32))


def hgrn(proj3, lower_bounds, hn2):
    b, seq, _ = proj3.shape
    n_hp = HG_WIDTH // LANES
    base = 3 * ATT_WIDTH // LANES

    def col(k):
        return pl.BlockSpec((None, seq, LANES), lambda bi, h, k=k: (bi, 0, base + k * n_hp + h),
                            pipeline_mode=pl.Buffered(1))
    return pl.pallas_call(
        functools.partial(_hgrn_kernel, seq=seq),
        grid=(b, n_hp),
        in_specs=[pl.BlockSpec((2, 2, LANES), lambda bi, h: (0, 0, h)),
                  col(0), col(1), col(2), col(3), col(4),
                  pl.BlockSpec((1, LANES), lambda bi, h: (0, 0))],
        out_specs=pl.BlockSpec((None, seq, LANES), lambda bi, h: (bi, 0, h)),
        out_shape=jax.ShapeDtypeStruct((b, seq, HG_WIDTH), bf16),
        scratch_shapes=[pltpu.VMEM((seq, LANES), f32)],
        compiler_params=_cparams(("arbitrary", "arbitrary")),
        name="hgrn2_bidir",
    )(lower_bounds, proj3, proj3, proj3, proj3, proj3, hn2)


OUT_TM = 512


def _outproj_kernel(x_ref, att_ref, hg_ref, w_ref, g_ref, h_ref, xn_ref):
    h = (x_ref[...]
         + jnp.dot(att_ref[...], w_ref[0:ATT_WIDTH, :], preferred_element_type=f32)
         + jnp.dot(hg_ref[...], w_ref[ATT_WIDTH:, :], preferred_element_type=f32))
    h_ref[...] = h
    ms = jnp.mean(h * h, axis=-1, keepdims=True)
    xn_ref[...] = (h * lax.rsqrt(ms + NORM_EPS)) * g_ref[...]


def out_proj(x2d, att2d, hg2d, w_bf16, gain):
    n = x2d.shape[0]
    row = lambda w: pl.BlockSpec((OUT_TM, w), lambda i: (i, 0))
    return pl.pallas_call(
        _outproj_kernel,
        grid=(n // OUT_TM,),
        in_specs=[row(D_MODEL), row(ATT_WIDTH), row(HG_WIDTH),
                  pl.BlockSpec((D_MODEL, D_MODEL), lambda i: (0, 0)),
                  pl.BlockSpec((1, D_MODEL), lambda i: (0, 0))],
        out_specs=[row(D_MODEL), row(D_MODEL)],
        out_shape=[jax.ShapeDtypeStruct((n, D_MODEL), f32)] * 2,
        compiler_params=_cparams(("arbitrary",)),
        name="out_proj",
    )(x2d, att2d, hg2d, w_bf16, gain)


RT_T = 256
PEER_TT = 128
PEER_J = PEER_HEADS * PEER_TOPK
BIG = 1.0e9


def _top16(s, posf):
    vals, poss = [], []
    for _ in range(PEER_TOPK):
        m = jnp.max(s, axis=0, keepdims=True)
        pos = jnp.min(jnp.where(s == m, posf, BIG), axis=0, keepdims=True)
        vals.append(m)
        poss.append(pos)
        s = jnp.where(posf == pos, NEG_INF, s)
    return vals, poss


def _stack16(rows):
    r16 = lax.broadcasted_iota(i32, (PEER_TOPK, rows[0].shape[1]), 0)
    out = jnp.zeros((PEER_TOPK, rows[0].shape[1]), f32)
    for kk, rw in enumerate(rows):
        out = jnp.where(r16 == kk, rw, out)
    return out


def _route_head(s0, s1):
    cdim = s0.shape[1]
    keypos = lax.broadcasted_iota(i32, (PEER_N_KEYS, cdim), 0).astype(f32)
    v0, p0 = _top16(s0, keypos)
    v1, p1 = _top16(s1, keypos)
    sv0, sp0, sv1, sp1 = _stack16(v0), _stack16(p0), _stack16(v1), _stack16(p1)

    r8 = lax.broadcasted_iota(i32, (SUBLANES, cdim), 0)
    groups_v = [v0[0] + sv1]
    groups_e = [p0[0] * PEER_N_KEYS + sp1]
    groups_p = [lax.broadcasted_iota(i32, (PEER_TOPK, cdim), 0).astype(f32)]
    for i in range(1, 8):
        nj = PEER_TOPK // (i + 1)
        ok = r8 < nj
        groups_v.append(jnp.where(ok, v0[i] + sv1[0:SUBLANES], NEG_INF))
        groups_e.append(p0[i] * PEER_N_KEYS + sp1[0:SUBLANES])
        groups_p.append((r8 + i * PEER_TOPK).astype(f32))
    groups_v.append(sv0[SUBLANES:] + v1[0])
    groups_e.append(sp0[SUBLANES:] * PEER_N_KEYS + p1[0])
    groups_p.append(((r8 + SUBLANES) * PEER_TOPK).astype(f32))
    cand = jnp.concatenate(groups_v, axis=0)
    cexp = jnp.concatenate(groups_e, axis=0)
    cpos = jnp.concatenate(groups_p, axis=0)

    vals, exps = [], []
    for _ in range(PEER_TOPK):
        m = jnp.max(cand, axis=0, keepdims=True)
        pos = jnp.min(jnp.where(cand == m, cpos, BIG), axis=0, keepdims=True)
        sel = cpos == pos
        exps.append(jnp.max(jnp.where(sel, cexp, -1.0), axis=0, keepdims=True))
        vals.append(m)
        cand = jnp.where(sel, NEG_INF, cand)
    sv = _stack16(vals)
    ex = jnp.exp(sv - vals[0])
    gate = ex / jnp.sum(ex, axis=0, keepdims=True)
    return _stack16(exps), gate


def _route_kernel(xn_ref, wqt_ref, keys_ref, idx_ref, gate_ref, qt_scr):
    xb = xn_ref[...].astype(bf16)
    qt_scr[...] = lax.dot_general(wqt_ref[...], xb, (((1,), (1,)), ((), ())),
                                  preferred_element_type=f32)

    def head_body(h, carry):
        for ci in range(RT_T // PEER_TT):
            cols = slice(ci * PEER_TT, (ci + 1) * PEER_TT)
            halves = []
            for p in range(2):
                r0 = pl.multiple_of((2 * h + p) * PEER_HALF, PEER_HALF)
                qh = qt_scr[pl.ds(r0, PEER_HALF), cols].astype(bf16)
                halves.append(jnp.dot(keys_ref[2 * h + p], qh, preferred_element_type=f32))
            eid, gate = _route_head(halves[0], halves[1])
            rows = pl.ds(pl.multiple_of(h * PEER_TOPK, PEER_TOPK), PEER_TOPK)
            idx_ref[ci, rows, :] = eid.astype(i32)
            gate_ref[ci, rows, :] = gate
        return carry
    lax.fori_loop(0, PEER_HEADS, head_body, 0)


def route(xn2d, wqt_bf16, keys_bf16):
    n = xn2d.shape[0]
    n_tiles = n // PEER_TT
    per = RT_T // PEER_TT
    ospec = pl.BlockSpec((per, PEER_J, PEER_TT), lambda i: (i, 0, 0))
    return pl.pallas_call(
        _route_kernel,
        grid=(n // RT_T,),
        in_specs=[pl.BlockSpec((RT_T, D_MODEL), lambda i: (i, 0)),
                  pl.BlockSpec((2 * PEER_HEADS * PEER_HALF, D_MODEL), lambda i: (0, 0)),
                  pl.BlockSpec((2 * PEER_HEADS, PEER_N_KEYS, PEER_HALF), lambda i: (0, 0, 0))],
        out_specs=[ospec, ospec],
        out_shape=[jax.ShapeDtypeStruct((n_tiles, PEER_J, PEER_TT), i32),
                   jax.ShapeDtypeStruct((n_tiles, PEER_J, PEER_TT), f32)],
        scratch_shapes=[pltpu.VMEM((2 * PEER_HEADS * PEER_HALF, RT_T), f32)],
        compiler_params=_cparams(("arbitrary",)),
        name="peer_route",
    )(xn2d, wqt_bf16, keys_bf16)


def _tile_copy(src_hbm, dst_smem, sem, tile, slot):
    return pltpu.make_async_copy(src_hbm.at[tile], dst_smem.at[slot], sem.at[slot])


def _fetch_tiles(step, n_steps, pairs):
    slot = step % 2

    @pl.when(step == 0)
    def _():
        for src, dst, sem in pairs:
            _tile_copy(src, dst, sem, 0, 0).start()

    for src, dst, sem in pairs:
        _tile_copy(src, dst, sem, step, slot).wait()

    @pl.when(step + 1 < n_steps)
    def _():
        for src, dst, sem in pairs:
            _tile_copy(src, dst, sem, step + 1, 1 - slot).start()
    return slot


def _peer_u_kernel(idx_hbm, tab_ref, x_ref, gate_ref, w_ref, idx_smem, sem, h_scr):
    step = pl.program_id(0)
    slot = _fetch_tiles(step, pl.num_programs(0), [(idx_hbm, idx_smem, sem)])
    h_scr[...] = jnp.zeros((PEER_J, PEER_TT), f32)
    lane = lax.broadcasted_iota(i32, (SUBLANES, PEER_TT), 1)
    sub = lax.broadcasted_iota(i32, (SUBLANES, LANES), 0)

    def tok_body(t, carry):
        xt = x_ref[t]
        for g in range(PEER_J // SUBLANES):
            acc = jnp.zeros((SUBLANES, LANES), f32)
            for jj in range(SUBLANES):
                e = idx_smem[slot, g * SUBLANES + jj, t]
                prod = tab_ref[e].astype(f32) * xt
                part = jnp.sum(prod, axis=0, keepdims=True)
                acc = jnp.where(sub == jj, part, acc)
            col = jnp.sum(acc, axis=-1, keepdims=True)
            rows = slice(g * SUBLANES, (g + 1) * SUBLANES)
            h_scr[rows, :] = jnp.where(lane == t, col, h_scr[rows, :])
        return carry
    lax.fori_loop(0, PEER_TT, tok_body, 0)
    w_ref[...] = gate_ref[...] * jax.nn.gelu(h_scr[...])


def peer_u_pass(idx3, gate3, tab3, xn3):
    n_tiles = idx3.shape[0]
    tspec = pl.BlockSpec((None, PEER_J, PEER_TT), lambda i: (i, 0, 0))
    return pl.pallas_call(
        _peer_u_kernel,
        grid=(n_tiles,),
        in_specs=[pl.BlockSpec(memory_space=pl.ANY),
                  pl.BlockSpec(memory_space=pltpu.VMEM),
                  pl.BlockSpec((PEER_TT, SUBLANES, LANES), lambda i: (i, 0, 0)),
                  tspec],
        out_specs=tspec,
        out_shape=jax.ShapeDtypeStruct((n_tiles, PEER_J, PEER_TT), f32),
        scratch_shapes=[pltpu.SMEM((2, PEER_J, PEER_TT), i32),
                        pltpu.SemaphoreType.DMA((2,)),
                        pltpu.VMEM((PEER_J, PEER_TT), f32)],
        compiler_params=_cparams(("arbitrary",)),
        name="peer_u",
    )(idx3, tab3, xn3, gate3)


PEER_NACC = 4


def _peer_v_kernel(idx_hbm, w_hbm, tab_ref, h_ref, y_ref, idx_smem, w_smem, sem_i, sem_w):
    step = pl.program_id(0)
    slot = _fetch_tiles(step, pl.num_programs(0),
                        [(idx_hbm, idx_smem, sem_i), (w_hbm, w_smem, sem_w)])

    def tok_body(t, carry):
        accs = [h_ref[t]] + [jnp.zeros((SUBLANES, LANES), f32)] * (PEER_NACC - 1)
        for j in range(PEER_J):
            e = idx_smem[slot, j, t]
            wt = w_smem[slot, j, t]
            accs[j % PEER_NACC] = accs[j % PEER_NACC] + wt * tab_ref[e].astype(f32)
        y_ref[t] = (accs[0] + accs[1]) + (accs[2] + accs[3])
        return carry
    lax.fori_loop(0, PEER_TT, tok_body, 0)


def peer_v_pass(idx3, w3, tab3, h3):
    n_tiles = idx3.shape[0]
    xspec = pl.BlockSpec((PEER_TT, SUBLANES, LANES), lambda i: (i, 0, 0))
    return pl.pallas_call(
        _peer_v_kernel,
        grid=(n_tiles,),
        in_specs=[pl.BlockSpec(memory_space=pl.ANY),
                  pl.BlockSpec(memory_space=pl.ANY),
                  pl.BlockSpec(memory_space=pltpu.VMEM),
                  xspec],
        out_specs=xspec,
        out_shape=jax.ShapeDtypeStruct(h3.shape, f32),
        scratch_shapes=[pltpu.SMEM((2, PEER_J, PEER_TT), i32),
                        pltpu.SMEM((2, PEER_J, PEER_TT), f32),
                        pltpu.SemaphoreType.DMA((2,)),
                        pltpu.SemaphoreType.DMA((2,))],
        compiler_params=_cparams(("arbitrary",)),
        name="peer_v",
    )(idx3, w3, tab3, h3)


def _trunk(x, p):
    b, seq, d = x.shape
    n = b * seq
    x2d = x.reshape(n, d)
    proj3 = in_proj(x2d, p["norm1"], p["w_in"]).reshape(b, seq, N_IN_COLS)
    att = attention(proj3, p["slopes"], p["q_gain"], p["k_gain"])
    hg = hgrn(proj3, p["lower_bounds"], p["hg_gain"])
    h1, xn2 = out_proj(x2d, att.reshape(n, ATT_WIDTH), hg.reshape(n, HG_WIDTH), p["w_out"], p["norm2"])
    idx3, gate3 = route(xn2, p["wq_t"], p["keys"])
    w3 = peer_u_pass(idx3, gate3, p["u_tab"], xn2.reshape(n, SUBLANES, LANES))
    y3 = peer_v_pass(idx3, w3, p["v_tab"], h1.reshape(n, SUBLANES, LANES))
    return y3.reshape(b, seq, d)


def kernel(x_prompt, x_sample, norm1, w_in, q_norm, k_norm, lower_bounds, hg_norm, w_out, norm2,
           peer_wq, peer_keys, peer_u, peer_v):
    n_heads = ATT_WIDTH // HEAD_DIM
    n_exp = peer_u.shape[1]
    p = {
        "norm1": norm1[0].reshape(1, D_MODEL),
        "w_in": w_in[0].astype(bf16),
        "q_gain": jnp.tile(q_norm[0], 2).reshape(1, LANES),
        "k_gain": jnp.tile(k_norm[0], 2).reshape(1, LANES),
        "hg_gain": jnp.tile(hg_norm[0], 2).reshape(1, LANES),
        "slopes": 2.0 ** (-8.0 * jnp.arange(1, n_heads + 1, dtype=f32) / n_heads),
        "lower_bounds": lower_bounds.astype(f32),
        "w_out": w_out[0].astype(bf16),
        "norm2": norm2[0].reshape(1, D_MODEL),
        "wq_t": peer_wq[0].T.astype(bf16),
        "keys": peer_keys[0].reshape(2 * PEER_HEADS, PEER_N_KEYS, PEER_HALF).astype(bf16),
        "u_tab": peer_u[0].astype(bf16).reshape(n_exp, SUBLANES, LANES),
        "v_tab": peer_v[0].astype(bf16).reshape(n_exp, SUBLANES, LANES),
    }
    return (_trunk(x_prompt, p), _trunk(x_sample, p))
```

```python
import functools

import jax
import jax.numpy as jnp
from jax import lax
from jax.experimental import pallas as pl
from jax.experimental.pallas import tpu as pltpu

f32 = jnp.float32
bf16 = jnp.bfloat16
i32 = jnp.int32

D_MODEL = 1024
ATT_WIDTH = 512
HG_WIDTH = 512
HEAD_DIM = 64
N_IN_COLS = 3 * ATT_WIDTH + 5 * HG_WIDTH
DILATIONS = (1, 4, 16)
KEYS_PER_SIDE = 64
HG_CHUNK = 64
PEER_HEADS = 8
PEER_N_KEYS = 128
PEER_TOPK = 16
PEER_HALF = 128
NORM_EPS = 1e-6

LANES = 128
SUBLANES = 8
VMEM_LIMIT = 56 * 1024 * 1024
NEG_INF = float("-inf")


def _cparams(sem=None, flags=None):
    return pltpu.CompilerParams(dimension_semantics=sem, vmem_limit_bytes=VMEM_LIMIT, flags=flags)


IN_TM = 512


def _inproj_kernel(x_ref, g_ref, w_ref, o_ref):
    x = x_ref[...]
    ms = jnp.mean(x * x, axis=-1, keepdims=True)
    xn = ((x * lax.rsqrt(ms + NORM_EPS)) * g_ref[...]).astype(bf16)
    tn = 1024
    for c in range(N_IN_COLS // tn):
        o_ref[:, c * tn:(c + 1) * tn] = jnp.dot(xn, w_ref[:, c * tn:(c + 1) * tn],
                                                preferred_element_type=f32)


def in_proj(x2d, gain, w_bf16):
    n = x2d.shape[0]
    return pl.pallas_call(
        _inproj_kernel,
        grid=(n // IN_TM,),
        in_specs=[pl.BlockSpec((IN_TM, D_MODEL), lambda i: (i, 0)),
                  pl.BlockSpec((1, D_MODEL), lambda i: (0, 0)),
                  pl.BlockSpec((D_MODEL, N_IN_COLS), lambda i: (0, 0))],
        out_specs=pl.BlockSpec((IN_TM, N_IN_COLS), lambda i: (i, 0)),
        out_shape=jax.ShapeDtypeStruct((n, N_IN_COLS), f32),
        compiler_params=_cparams(("arbitrary",)),
        name="in_proj",
    )(x2d, gain, w_bf16)


def _pair_rmsnorm(x, gain, head0):
    x2 = x * x
    s0 = jnp.sum(jnp.where(head0, x2, 0.0), axis=-1, keepdims=True)
    s1 = jnp.sum(jnp.where(head0, 0.0, x2), axis=-1, keepdims=True)
    ms = jnp.where(head0, s0, s1) * (1.0 / HEAD_DIM)
    return (x * lax.rsqrt(ms + NORM_EPS)) * gain


ATT_QB = 128
ATT_KW = ATT_QB + 2 * KEYS_PER_SIDE
ATT_CP = 256


def _attn_kernel(slopes_ref, q_ref, k_ref, v_ref, qg_ref, kg_ref, o_ref,
                 qc, kc, vc, oacc, macc, lacc, *, seq):
    hp = pl.program_id(1)
    lane = lax.broadcasted_iota(i32, (1, LANES), 1)
    head0 = lane < HEAD_DIM
    qg = qg_ref[...]
    kg = kg_ref[...]
    zpad = jnp.zeros((KEYS_PER_SIDE, LANES), bf16)
    for ref in (qc, kc, vc):
        ref[0:KEYS_PER_SIDE, :] = zpad
        ref[seq + KEYS_PER_SIDE:seq + 2 * KEYS_PER_SIDE, :] = zpad
    slope = (slopes_ref[2 * hp], slopes_ref[2 * hp + 1])

    iq = lax.broadcasted_iota(i32, (ATT_QB, ATT_KW), 0)
    jj = lax.broadcasted_iota(i32, (ATT_QB, ATT_KW), 1)
    rel = jnp.abs(iq + KEYS_PER_SIDE - jj)
    band = rel <= KEYS_PER_SIDE

    for pat, dil in enumerate(DILATIONS):
        cls_len = seq // dil
        shift = cls_len.bit_length() - 1
        cp = min(ATT_CP, cls_len)
        n_cp = cls_len // cp

        for r in range(dil):
            def copy_body(c, carry, r=r, cp=cp):
                if dil == 1:
                    src = pl.ds(pl.multiple_of(c * cp, cp), cp)
                else:
                    src = pl.ds(r + dil * cp * c, cp, stride=dil)
                dst = pl.ds(pl.multiple_of(KEYS_PER_SIDE + r * cls_len + c * cp, KEYS_PER_SIDE), cp)
                qn = _pair_rmsnorm(q_ref[src, :], qg, head0) * (HEAD_DIM ** -0.5)
                qc[dst, :] = qn.astype(bf16)
                kc[dst, :] = _pair_rmsnorm(k_ref[src, :], kg, head0).astype(bf16)
                vc[dst, :] = v_ref[src, :].astype(bf16)
                return carry
            lax.fori_loop(0, n_cp, copy_body, 0)

        relf = (dil * rel).astype(f32)
        bias = [jnp.where(band, -slope[a] * relf, NEG_INF) for a in range(2)]

        def block_body(i, carry, pat=pat, dil=dil, cls_len=cls_len, shift=shift, bias=bias):
            r0 = pl.multiple_of(i * ATT_QB, ATT_QB)
            cls = lax.shift_right_logical(r0, shift)
            lo = cls * cls_len - (r0 - KEYS_PER_SIDE)
            valid = (jj >= lo) & (jj < lo + cls_len)
            qb = qc[pl.ds(pl.multiple_of(r0 + KEYS_PER_SIDE, KEYS_PER_SIDE), ATT_QB), :]
            kw = kc[pl.ds(r0, ATT_KW), :]
            vw = vc[pl.ds(r0, ATT_KW), :]
            res = []
            for a in range(2):
                hm = head0 if a == 0 else jnp.logical_not(head0)
                qa = jnp.where(hm, qb, jnp.zeros_like(qb))
                s = lax.dot_general(qa, kw, (((1,), (1,)), ((), ())), preferred_element_type=f32)
                s = jnp.where(valid, s + bias[a], NEG_INF)
                m = jnp.max(s, axis=-1, keepdims=True)
                p = jnp.exp(s - m)
                l = jnp.sum(p, axis=-1, keepdims=True)
                o = jnp.dot(p.astype(bf16), vw, preferred_element_type=f32)
                res.append((o, m, l))
            o = jnp.where(head0, res[0][0], res[1][0])
            m = jnp.where(head0, res[0][1], res[1][1])
            l = jnp.where(head0, res[0][2], res[1][2])
            if dil == 1:
                tok = pl.ds(r0, ATT_QB)
            else:
                tok = pl.ds((r0 - cls * cls_len) * dil + cls, ATT_QB, stride=dil)
            if pat == 0:
                oacc[tok, :] = o
                macc[tok, :] = m
                lacc[tok, :] = l
            else:
                m_old = macc[tok, :]
                m_new = jnp.maximum(m_old, m)
                a_old = jnp.exp(m_old - m_new)
                a_new = jnp.exp(m - m_new)
                oacc[tok, :] = a_old * oacc[tok, :] + a_new * o
                lacc[tok, :] = a_old * lacc[tok, :] + a_new * l
                macc[tok, :] = m_new
            return carry
        lax.fori_loop(0, seq // ATT_QB, block_body, 0)

    def fin_body(c, carry):
        rows = pl.ds(pl.multiple_of(c * ATT_CP, ATT_CP), ATT_CP)
        o_ref[rows, :] = (oacc[rows, :] / lacc[rows, :]).astype(bf16)
        return carry
    lax.fori_loop(0, seq // ATT_CP, fin_body, 0)


def attention(proj3, slopes, qg2, kg2):
    b, seq, _ = proj3.shape
    n_hp = ATT_WIDTH // LANES

    def col(off):
        return pl.BlockSpec((None, seq, LANES), lambda bi, h, off=off: (bi, 0, off + h),
                            pipeline_mode=pl.Buffered(1))
    gspec = pl.BlockSpec((1, LANES), lambda bi, h: (0, 0))
    return pl.pallas_call(
        functools.partial(_attn_kernel, seq=seq),
        grid=(b, n_hp),
        in_specs=[pl.BlockSpec(memory_space=pltpu.SMEM),
                  col(0), col(n_hp), col(2 * n_hp), gspec, gspec],
        out_specs=pl.BlockSpec((None, seq, LANES), lambda bi, h: (bi, 0, h)),
        out_shape=jax.ShapeDtypeStruct((b, seq, ATT_WIDTH), bf16),
        scratch_shapes=[pltpu.VMEM((seq + 2 * KEYS_PER_SIDE, LANES), bf16)] * 3
                       + [pltpu.VMEM((seq, LANES), f32)] * 3,
        compiler_params=_cparams(("arbitrary", "arbitrary")),
        name="dilated_attention",
    )(slopes, proj3, proj3, proj3, qg2, kg2)


HG_LEVELS = 6


def _shift_rows(x, j, up):
    n = x.shape[0]
    return pltpu.roll(x, (n - j) % n if up else j, axis=0)


def _hg_chunk(q, k, v, lf, st, rev, head0_rows, blockdiag):
    c = HG_CHUNK
    row = lax.broadcasted_iota(i32, (c, LANES), 0)
    cum = lf
    for lv in range(HG_LEVELS):
        j = 1 << lv
        sh = _shift_rows(cum, j, up=rev)
        ok = (row < c - j) if rev else (row >= j)
        cum = cum + jnp.where(ok, sh, 0.0)

    trow = lax.broadcasted_iota(i32, (2 * c, c), 0) & (c - 1)
    scol = lax.broadcasted_iota(i32, (2 * c, c), 1)

    def pair_scores(a_mat, b_mat, lv):
        a2 = jnp.concatenate([jnp.where(head0_rows, a_mat, 0.0), jnp.where(head0_rows, 0.0, a_mat)], axis=0)
        sc = lax.dot_general(a2.astype(bf16), b_mat.astype(bf16), (((1,), (1,)), ((), ())),
                             preferred_element_type=f32)
        same = lax.shift_right_logical(trow, lv) == lax.shift_right_logical(scol, lv)
        return jnp.where(same, sc, 0.0)

    scores = pair_scores(q, k, 0)
    edge = cum
    for lv in range(1, HG_LEVELS + 1):
        h = 1 << (lv - 1)
        if h > 1:
            hh = h >> 1
            sh = _shift_rows(edge, hh, up=not rev)
            take = ((row & hh) == 0) != rev
            edge = jnp.where(take, sh, edge)
        far = ((row & h) != 0) != rev
        edge_prev = _shift_rows(edge, h, up=rev)
        a_mat = q * jnp.exp(jnp.where(far, cum - edge_prev, NEG_INF))
        b_mat = k * jnp.exp(jnp.where(far, NEG_INF, edge - cum))
        scores = scores + pair_scores(a_mat, b_mat, lv)

    hh = c >> 1
    sh = _shift_rows(edge, hh, up=not rev)
    total = jnp.where(((row & hh) == 0) != rev, sh, edge)

    vb = v.astype(bf16)
    scb = scores.astype(bf16)
    intra0 = jnp.dot(scb[0:c], vb, preferred_element_type=f32)
    intra1 = jnp.dot(scb[c:2 * c], vb, preferred_element_type=f32)
    intra = jnp.where(head0_rows, intra0, intra1)
    qe = (q * jnp.exp(cum)).astype(bf16)
    inter = lax.dot_general(qe, st.astype(bf16), (((1,), (1,)), ((), ())), preferred_element_type=f32)
    kd = (k * jnp.exp(total - cum)).astype(bf16)
    upd = lax.dot_general(vb, kd, (((0,), (0,)), ((), ())), preferred_element_type=f32)
    st_new = st * jnp.exp(total[0:1, :]) + jnp.where(blockdiag, upd, 0.0)
    return inter + intra, st_new


def _hgrn_kernel(lb_ref, q_ref, zf_ref, zb_ref, i_ref, g_ref, hn_ref, o_ref, of_scr, *, seq):
    c = HG_CHUNK
    lane = lax.broadcasted_iota(i32, (1, LANES), 1)
    head0 = lane < HEAD_DIM
    head0_rows = lax.broadcasted_iota(i32, (c, LANES), 1) < HEAD_DIM
    r2 = lax.broadcasted_iota(i32, (LANES, LANES), 0) < HEAD_DIM
    c2 = lax.broadcasted_iota(i32, (LANES, LANES), 1) < HEAD_DIM
    blockdiag = r2 == c2

    lbs = []
    for d in range(2):
        a0 = lb_ref[d, 0:1, :]
        a1 = lb_ref[d, 1:2, :]
        mx = jnp.maximum(a0, a1)
        e0 = jnp.exp(a0 - mx)
        e1 = jnp.exp(a1 - mx)
        lbs.append(e0 / (e0 + e1))

    n_chunks = seq // c

    def gates(z, lb):
        f = lb + (1.0 - lb) * jax.nn.sigmoid(z)
        return jnp.log(f), 1.0 - f

    def fwd_body(ci, st):
        rows = pl.ds(pl.multiple_of(ci * c, c), c)
        lf, k = gates(zf_ref[rows, :], lbs[0])
        o, st = _hg_chunk(q_ref[rows, :], k, i_ref[rows, :], lf, st, False, head0_rows, blockdiag)
        of_scr[rows, :] = o
        return st
    lax.fori_loop(0, n_chunks, fwd_body, jnp.zeros((LANES, LANES), f32))

    hn = hn_ref[...]

    def bwd_body(cj, st):
        ci = n_chunks - 1 - cj
        rows = pl.ds(pl.multiple_of(ci * c, c), c)
        lf, k = gates(zb_ref[rows, :], lbs[1])
        o, st = _hg_chunk(q_ref[rows, :], k, i_ref[rows, :], lf, st, True, head0_rows, blockdiag)
        tot = of_scr[rows, :] + o
        y = _pair_rmsnorm(tot, hn, head0)
        o_ref[rows, :] = (y * jax.nn.silu(g_ref[rows, :])).astype(bf16)
        return st
    lax.fori_loop(0, n_chunks, bwd_body, jnp.zeros((LANES, LANES), f32))


def hgrn(proj3, lower_bounds, hn2):
    b, seq, _ = proj3.shape
    n_hp = HG_WIDTH // LANES
    base = 3 * ATT_WIDTH // LANES

    def col(k):
        return pl.BlockSpec((None, seq, LANES), lambda bi, h, k=k: (bi, 0, base + k * n_hp + h),
                            pipeline_mode=pl.Buffered(1))
    return pl.pallas_call(
        functools.partial(_hgrn_kernel, seq=seq),
        grid=(b, n_hp),
        in_specs=[pl.BlockSpec((2, 2, LANES), lambda bi, h: (0, 0, h)),
                  col(0), col(1), col(2), col(3), col(4),
                  pl.BlockSpec((1, LANES), lambda bi, h: (0, 0))],
        out_specs=pl.BlockSpec((None, seq, LANES), lambda bi, h: (bi, 0, h)),
        out_shape=jax.ShapeDtypeStruct((b, seq, HG_WIDTH), bf16),
        scratch_shapes=[pltpu.VMEM((seq, LANES), f32)],
        compiler_params=_cparams(("arbitrary", "arbitrary")),
        name="hgrn2_bidir",
    )(lower_bounds, proj3, proj3, proj3, proj3, proj3, hn2)


OUT_TM = 512


def _outproj_kernel(x_ref, att_ref, hg_ref, w_ref, g_ref, h_ref, xn_ref):
    h = (x_ref[...]
         + jnp.dot(att_ref[...], w_ref[0:ATT_WIDTH, :], preferred_element_type=f32)
         + jnp.dot(hg_ref[...], w_ref[ATT_WIDTH:, :], preferred_element_type=f32))
    h_ref[...] = h
    ms = jnp.mean(h * h, axis=-1, keepdims=True)
    xn_ref[...] = (h * lax.rsqrt(ms + NORM_EPS)) * g_ref[...]


def out_proj(x2d, att2d, hg2d, w_bf16, gain):
    n = x2d.shape[0]
    row = lambda w: pl.BlockSpec((OUT_TM, w), lambda i: (i, 0))
    return pl.pallas_call(
        _outproj_kernel,
        grid=(n // OUT_TM,),
        in_specs=[row(D_MODEL), row(ATT_WIDTH), row(HG_WIDTH),
                  pl.BlockSpec((D_MODEL, D_MODEL), lambda i: (0, 0)),
                  pl.BlockSpec((1, D_MODEL), lambda i: (0, 0))],
        out_specs=[row(D_MODEL), row(D_MODEL)],
        out_shape=[jax.ShapeDtypeStruct((n, D_MODEL), f32)] * 2,
        compiler_params=_cparams(("arbitrary",)),
        name="out_proj",
    )(x2d, att2d, hg2d, w_bf16, gain)


RT_T = 256
PEER_TT = 128
PEER_J = PEER_HEADS * PEER_TOPK
BIG = 1.0e9


def _top16(s, posf):
    vals, poss = [], []
    for _ in range(PEER_TOPK):
        m = jnp.max(s, axis=0, keepdims=True)
        pos = jnp.min(jnp.where(s == m, posf, BIG), axis=0, keepdims=True)
        vals.append(m)
        poss.append(pos)
        s = jnp.where(posf == pos, NEG_INF, s)
    return vals, poss


def _stack16(rows):
    r16 = lax.broadcasted_iota(i32, (PEER_TOPK, rows[0].shape[1]), 0)
    out = jnp.zeros((PEER_TOPK, rows[0].shape[1]), f32)
    for kk, rw in enumerate(rows):
        out = jnp.where(r16 == kk, rw, out)
    return out


def _route_head(s0, s1):
    cdim = s0.shape[1]
    keypos = lax.broadcasted_iota(i32, (PEER_N_KEYS, cdim), 0).astype(f32)
    v0, p0 = _top16(s0, keypos)
    v1, p1 = _top16(s1, keypos)
    sv0, sp0, sv1, sp1 = _stack16(v0), _stack16(p0), _stack16(v1), _stack16(p1)

    r8 = lax.broadcasted_iota(i32, (SUBLANES, cdim), 0)
    groups_v = [v0[0] + sv1]
    groups_e = [p0[0] * PEER_N_KEYS + sp1]
    groups_p = [lax.broadcasted_iota(i32, (PEER_TOPK, cdim), 0).astype(f32)]
    for i in range(1, 8):
        nj = PEER_TOPK // (i + 1)
        ok = r8 < nj
        groups_v.append(jnp.where(ok, v0[i] + sv1[0:SUBLANES], NEG_INF))
        groups_e.append(p0[i] * PEER_N_KEYS + sp1[0:SUBLANES])
        groups_p.append((r8 + i * PEER_TOPK).astype(f32))
    groups_v.append(sv0[SUBLANES:] + v1[0])
    groups_e.append(sp0[SUBLANES:] * PEER_N_KEYS + p1[0])
    groups_p.append(((r8 + SUBLANES) * PEER_TOPK).astype(f32))
    cand = jnp.concatenate(groups_v, axis=0)
    cexp = jnp.concatenate(groups_e, axis=0)
    cpos = jnp.concatenate(groups_p, axis=0)

    vals, exps = [], []
    for _ in range(PEER_TOPK):
        m = jnp.max(cand, axis=0, keepdims=True)
        pos = jnp.min(jnp.where(cand == m, cpos, BIG), axis=0, keepdims=True)
        sel = cpos == pos
        exps.append(jnp.max(jnp.where(sel, cexp, -1.0), axis=0, keepdims=True))
        vals.append(m)
        cand = jnp.where(sel, NEG_INF, cand)
    sv = _stack16(vals)
    ex = jnp.exp(sv - vals[0])
    gate = ex / jnp.sum(ex, axis=0, keepdims=True)
    return _stack16(exps), gate


def _route_kernel(xn_ref, wqt_ref, keys_ref, idx_ref, gate_ref, qt_scr, eid_scr):
    xb = xn_ref[...].astype(bf16)
    qt_scr[...] = lax.dot_general(wqt_ref[...], xb, (((1,), (1,)), ((), ())),
                                  preferred_element_type=f32)

    def head_body(h, carry):
        for ci in range(RT_T // PEER_TT):
            cols = slice(ci * PEER_TT, (ci + 1) * PEER_TT)
            halves = []
            for p in range(2):
                r0 = pl.multiple_of((2 * h + p) * PEER_HALF, PEER_HALF)
                qh = qt_scr[pl.ds(r0, PEER_HALF), cols].astype(bf16)
                halves.append(jnp.dot(keys_ref[2 * h + p], qh, preferred_element_type=f32))
            eid, gate = _route_head(halves[0], halves[1])
            rows = pl.ds(pl.multiple_of(h * PEER_TOPK, PEER_TOPK), PEER_TOPK)
            eid_scr[ci, rows, :] = eid
            gate_ref[ci, rows, :] = gate
        return carry
    lax.fori_loop(0, PEER_HEADS, head_body, 0)
    for ci in range(RT_T // PEER_TT):
        idx_ref[ci] = eid_scr[ci].T.astype(i32)


def route(xn2d, wqt_bf16, keys_bf16):
    n = xn2d.shape[0]
    n_tiles = n // PEER_TT
    per = RT_T // PEER_TT
    ospec = pl.BlockSpec((per, PEER_J, PEER_TT), lambda i: (i, 0, 0))
    return pl.pallas_call(
        _route_kernel,
        grid=(n // RT_T,),
        in_specs=[pl.BlockSpec((RT_T, D_MODEL), lambda i: (i, 0)),
                  pl.BlockSpec((2 * PEER_HEADS * PEER_HALF, D_MODEL), lambda i: (0, 0)),
                  pl.BlockSpec((2 * PEER_HEADS, PEER_N_KEYS, PEER_HALF), lambda i: (0, 0, 0))],
        out_specs=[pl.BlockSpec((per, PEER_TT, PEER_J), lambda i: (i, 0, 0)), ospec],
        out_shape=[jax.ShapeDtypeStruct((n_tiles, PEER_TT, PEER_J), i32),
                   jax.ShapeDtypeStruct((n_tiles, PEER_J, PEER_TT), f32)],
        scratch_shapes=[pltpu.VMEM((2 * PEER_HEADS * PEER_HALF, RT_T), f32),
                        pltpu.VMEM((per, PEER_J, PEER_TT), f32)],
        compiler_params=_cparams(("arbitrary",)),
        name="peer_route",
    )(xn2d, wqt_bf16, keys_bf16)


PEER_FLAT = PEER_TT * PEER_J
PEER_NACC = 4


def _resident_tile_offset(step, n_steps, idx_hbm, idx_smem, sem):
    def copy(tile, slot):
        dst = idx_smem.at[pl.ds(pl.multiple_of(slot * PEER_FLAT, PEER_FLAT), PEER_FLAT)]
        return pltpu.make_async_copy(idx_hbm.at[tile], dst, sem.at[slot])

    slot = step % 2

    @pl.when(step == 0)
    def _():
        copy(0, 0).start()

    copy(step, slot).wait()

    @pl.when(step + 1 < n_steps)
    def _():
        copy(step + 1, 1 - slot).start()
    return slot * PEER_FLAT


def _sublane_fold(x, y, s, sub):
    clear = (sub & s) == 0
    a = jnp.where(clear, x, y)
    b = jnp.where(clear, y, x)
    if 2 * s == SUBLANES:
        partner = pltpu.roll(b, s, axis=0)
    else:
        partner = jnp.where(clear, pltpu.roll(b, SUBLANES - s, axis=0), pltpu.roll(b, s, axis=0))
    return a + partner


_BITREV3 = (0, 4, 2, 6, 1, 5, 3, 7)
PEER_GROUPS = PEER_J // SUBLANES


def _peer_u_kernel(idx_hbm, tab_ref, x_ref, gate_ref, w_ref, idx_smem, sem, h_scr):
    off = _resident_tile_offset(pl.program_id(0), pl.num_programs(0), idx_hbm, idx_smem, sem)
    lane = lax.broadcasted_iota(i32, (SUBLANES, PEER_TT), 1)
    sub = lax.broadcasted_iota(i32, (SUBLANES, LANES), 0)
    h_scr[...] = jnp.zeros((PEER_J, PEER_TT), f32)

    def retire(t_done, partial):
        for g in range(PEER_GROUPS):
            col = jnp.sum(partial[g], axis=-1, keepdims=True)
            rows = slice(g * SUBLANES, (g + 1) * SUBLANES)
            h_scr[rows, :] = jnp.where(lane == t_done, col, h_scr[rows, :])

    def tok_body(t, prev):
        retire(t - 1, prev)
        xt = x_ref[t]
        base = off + t * PEER_J
        partial = []
        for g in range(PEER_GROUPS):
            p = [tab_ref[idx_smem[base + g * SUBLANES + jj]].astype(f32) * xt for jj in _BITREV3]
            q = [_sublane_fold(p[2 * i], p[2 * i + 1], 4, sub) for i in range(4)]
            r = [_sublane_fold(q[2 * i], q[2 * i + 1], 2, sub) for i in range(2)]
            partial.append(_sublane_fold(r[0], r[1], 1, sub))
        return tuple(partial)
    zero = jnp.zeros((SUBLANES, LANES), f32)
    last = lax.fori_loop(0, PEER_TT, tok_body, (zero,) * PEER_GROUPS)
    retire(PEER_TT - 1, last)
    w_ref[...] = gate_ref[...] * jax.nn.gelu(h_scr[...])


def peer_u_pass(idx2, gate3, tab3, xn3):
    n_tiles = idx2.shape[0]
    tspec = pl.BlockSpec((None, PEER_J, PEER_TT), lambda i: (i, 0, 0))
    return pl.pallas_call(
        _peer_u_kernel,
        grid=(n_tiles,),
        in_specs=[pl.BlockSpec(memory_space=pl.ANY),
                  pl.BlockSpec(memory_space=pltpu.VMEM),
                  pl.BlockSpec((PEER_TT, SUBLANES, LANES), lambda i: (i, 0, 0)),
                  tspec],
        out_specs=tspec,
        out_shape=jax.ShapeDtypeStruct((n_tiles, PEER_J, PEER_TT), f32),
        scratch_shapes=[pltpu.SMEM((2 * PEER_FLAT,), i32),
                        pltpu.SemaphoreType.DMA((2,)),
                        pltpu.VMEM((PEER_J, PEER_TT), f32)],
        compiler_params=_cparams(("arbitrary",)),
        name="peer_u",
    )(idx2, tab3, xn3, gate3)


def _peer_v_kernel(idx_hbm, tab_ref, w_ref, h_ref, y_ref, idx_smem, sem):
    off = _resident_tile_offset(pl.program_id(0), pl.num_programs(0), idx_hbm, idx_smem, sem)
    lane = lax.broadcasted_iota(i32, (SUBLANES, PEER_TT), 1)

    def lane_splat(t):
        out = []
        for g in range(PEER_GROUPS):
            wg = w_ref[g * SUBLANES:(g + 1) * SUBLANES, :]
            col = jnp.sum(jnp.where(lane == t, wg, 0.0), axis=-1, keepdims=True)
            out.append(jnp.broadcast_to(col, (SUBLANES, LANES)))
        return tuple(out)

    def tok_body(t, wsplat):
        nxt = lane_splat(t + 1)
        base = off + t * PEER_J
        zero = jnp.zeros((SUBLANES, LANES), f32)
        accs = [h_ref[t]] + [zero] * (PEER_NACC - 1)
        for g in range(PEER_GROUPS):
            p = []
            for i in range(SUBLANES):
                row = tab_ref[idx_smem[base + g * SUBLANES + i]].astype(f32)
                p.append(jnp.broadcast_to(wsplat[g][i:i + 1, :], (SUBLANES, LANES)) * row)
            s = ((p[0] + p[1]) + (p[2] + p[3])) + ((p[4] + p[5]) + (p[6] + p[7]))
            accs[g % PEER_NACC] = accs[g % PEER_NACC] + s
        y_ref[t] = (accs[0] + accs[1]) + (accs[2] + accs[3])
        return nxt
    lax.fori_loop(0, PEER_TT, tok_body, lane_splat(0))


def peer_v_pass(idx2, w3, tab3, h3):
    n_tiles = idx2.shape[0]
    xspec = pl.BlockSpec((PEER_TT, SUBLANES, LANES), lambda i: (i, 0, 0))
    return pl.pallas_call(
        _peer_v_kernel,
        grid=(n_tiles,),
        in_specs=[pl.BlockSpec(memory_space=pl.ANY),
                  pl.BlockSpec(memory_space=pltpu.VMEM),
                  pl.BlockSpec((None, PEER_J, PEER_TT), lambda i: (i, 0, 0)),
                  xspec],
        out_specs=xspec,
        out_shape=jax.ShapeDtypeStruct(h3.shape, f32),
        scratch_shapes=[pltpu.SMEM((2 * PEER_FLAT,), i32),
                        pltpu.SemaphoreType.DMA((2,))],
        compiler_params=_cparams(("arbitrary",)),
        name="peer_v",
    )(idx2, tab3, w3, h3)


def _trunk(x, p):
    b, seq, d = x.shape
    n = b * seq
    x2d = x.reshape(n, d)
    proj3 = in_proj(x2d, p["norm1"], p["w_in"]).reshape(b, seq, N_IN_COLS)
    att = attention(proj3, p["slopes"], p["q_gain"], p["k_gain"])
    hg = hgrn(proj3, p["lower_bounds"], p["hg_gain"])
    h1, xn2 = out_proj(x2d, att.reshape(n, ATT_WIDTH), hg.reshape(n, HG_WIDTH), p["w_out"], p["norm2"])
    idx3, gate3 = route(xn2, p["wq_t"], p["keys"])
    idx2 = idx3.reshape(-1, PEER_FLAT)
    w3 = peer_u_pass(idx2, gate3, p["u_tab"], xn2.reshape(n, SUBLANES, LANES))
    y3 = peer_v_pass(idx2, w3, p["v_tab"], h1.reshape(n, SUBLANES, LANES))
    return y3.reshape(b, seq, d)


def kernel(x_prompt, x_sample, norm1, w_in, q_norm, k_norm, lower_bounds, hg_norm, w_out, norm2,
           peer_wq, peer_keys, peer_u, peer_v):
    n_heads = ATT_WIDTH // HEAD_DIM
    n_exp = peer_u.shape[1]
    p = {
        "norm1": norm1[0].reshape(1, D_MODEL),
        "w_in": w_in[0].astype(bf16),
        "q_gain": jnp.tile(q_norm[0], 2).reshape(1, LANES),
        "k_gain": jnp.tile(k_norm[0], 2).reshape(1, LANES),
        "hg_gain": jnp.tile(hg_norm[0], 2).reshape(1, LANES),
        "slopes": 2.0 ** (-8.0 * jnp.arange(1, n_heads + 1, dtype=f32) / n_heads),
        "lower_bounds": lower_bounds.astype(f32),
        "w_out": w_out[0].astype(bf16),
        "norm2": norm2[0].reshape(1, D_MODEL),
        "wq_t": peer_wq[0].T.astype(bf16),
        "keys": peer_keys[0].reshape(2 * PEER_HEADS, PEER_N_KEYS, PEER_HALF).astype(bf16),
        "u_tab": peer_u[0].astype(bf16).reshape(n_exp, SUBLANES, LANES),
        "v_tab": peer_v[0].astype(bf16).reshape(n_exp, SUBLANES, LANES),
    }
    return (_trunk(x_prompt, p), _trunk(x_sample, p))
```

```python
import functools

import jax
import jax.numpy as jnp
from jax import lax
from jax.experimental import pallas as pl
from jax.experimental.pallas import tpu as pltpu

f32 = jnp.float32
bf16 = jnp.bfloat16
i32 = jnp.int32

D_MODEL = 1024
ATT_WIDTH = 512
HG_WIDTH = 512
HEAD_DIM = 64
N_IN_COLS = 3 * ATT_WIDTH + 5 * HG_WIDTH
DILATIONS = (1, 4, 16)
KEYS_PER_SIDE = 64
HG_CHUNK = 64
PEER_HEADS = 8
PEER_N_KEYS = 128
PEER_TOPK = 16
PEER_HALF = 128
NORM_EPS = 1e-6

LANES = 128
SUBLANES = 8
VMEM_LIMIT = 56 * 1024 * 1024
NEG_INF = float("-inf")


def _cparams(sem=None, flags=None):
    return pltpu.CompilerParams(dimension_semantics=sem, vmem_limit_bytes=VMEM_LIMIT, flags=flags)


IN_TM = 512


def _inproj_kernel(x_ref, g_ref, w_ref, o_ref):
    x = x_ref[...]
    ms = jnp.mean(x * x, axis=-1, keepdims=True)
    xn = ((x * lax.rsqrt(ms + NORM_EPS)) * g_ref[...]).astype(bf16)
    tn = 1024
    for c in range(N_IN_COLS // tn):
        o_ref[:, c * tn:(c + 1) * tn] = jnp.dot(xn, w_ref[:, c * tn:(c + 1) * tn],
                                                preferred_element_type=f32)


def in_proj(x2d, gain, w_bf16):
    n = x2d.shape[0]
    return pl.pallas_call(
        _inproj_kernel,
        grid=(n // IN_TM,),
        in_specs=[pl.BlockSpec((IN_TM, D_MODEL), lambda i: (i, 0)),
                  pl.BlockSpec((1, D_MODEL), lambda i: (0, 0)),
                  pl.BlockSpec((D_MODEL, N_IN_COLS), lambda i: (0, 0))],
        out_specs=pl.BlockSpec((IN_TM, N_IN_COLS), lambda i: (i, 0)),
        out_shape=jax.ShapeDtypeStruct((n, N_IN_COLS), f32),
        compiler_params=_cparams(("arbitrary",)),
        name="in_proj",
    )(x2d, gain, w_bf16)


def _pair_rmsnorm(x, gain, same_head):
    x2 = x * x
    hi = x2.astype(bf16)
    lo = (x2 - hi.astype(f32)).astype(bf16)
    ss = (jnp.dot(hi, same_head, preferred_element_type=f32)
          + jnp.dot(lo, same_head, preferred_element_type=f32))
    return (x * lax.rsqrt(ss * (1.0 / HEAD_DIM) + NORM_EPS)) * gain


def _same_head_matrix():
    r = lax.broadcasted_iota(i32, (LANES, LANES), 0) < HEAD_DIM
    c = lax.broadcasted_iota(i32, (LANES, LANES), 1) < HEAD_DIM
    return (r == c).astype(bf16)


ATT_QB = 128
ATT_KW = ATT_QB + 2 * KEYS_PER_SIDE
ATT_CP = 256
ATT_CPU = 2
ATT_NB = 8


def _attn_kernel(slopes_ref, q_ref, k_ref, v_ref, qg_ref, kg_ref, o_ref,
                 qc, kc, vc, oacc, macc, lacc, *, seq):
    hp = pl.program_id(1)
    lane = lax.broadcasted_iota(i32, (1, LANES), 1)
    head0 = lane < HEAD_DIM
    same_head = _same_head_matrix()
    qg = qg_ref[...]
    kg = kg_ref[...]
    zpad = jnp.zeros((KEYS_PER_SIDE, LANES), bf16)
    for ref in (qc, kc, vc):
        ref[0:KEYS_PER_SIDE, :] = zpad
        ref[seq + KEYS_PER_SIDE:seq + 2 * KEYS_PER_SIDE, :] = zpad
    slope = (slopes_ref[2 * hp], slopes_ref[2 * hp + 1])

    iq = lax.broadcasted_iota(i32, (ATT_QB, ATT_KW), 0)
    jj = lax.broadcasted_iota(i32, (ATT_QB, ATT_KW), 1)
    rel = jnp.abs(iq + KEYS_PER_SIDE - jj)
    band = rel <= KEYS_PER_SIDE

    for pat, dil in enumerate(DILATIONS):
        cls_len = seq // dil
        shift = cls_len.bit_length() - 1
        cp = min(ATT_CP, cls_len)
        n_cp = cls_len // cp

        def copy_body(i, carry, dil=dil, cp=cp, n_cp=n_cp):
            for u in range(ATT_CPU):
                chunk = ATT_CPU * i + u
                if dil == 1:
                    src = pl.ds(pl.multiple_of(chunk * cp, cp), cp)
                else:
                    r = lax.shift_right_logical(chunk, n_cp.bit_length() - 1)
                    c = chunk & (n_cp - 1)
                    src = pl.ds(r + dil * cp * c, cp, stride=dil)
                dst = pl.ds(pl.multiple_of(KEYS_PER_SIDE + chunk * cp, KEYS_PER_SIDE), cp)
                qn = _pair_rmsnorm(q_ref[src, :], qg, same_head) * (HEAD_DIM ** -0.5)
                qc[dst, :] = qn.astype(bf16)
                kc[dst, :] = _pair_rmsnorm(k_ref[src, :], kg, same_head).astype(bf16)
                vc[dst, :] = v_ref[src, :].astype(bf16)
            return carry
        lax.fori_loop(0, seq // (cp * ATT_CPU), copy_body, 0)

        relf = (dil * rel).astype(f32)
        bias = [jnp.where(band, -slope[a] * relf, NEG_INF) for a in range(2)]

        def one_block(i, dil=dil, cls_len=cls_len, shift=shift, bias=bias):
            r0 = pl.multiple_of(i * ATT_QB, ATT_QB)
            cls = lax.shift_right_logical(r0, shift)
            lo = cls * cls_len - (r0 - KEYS_PER_SIDE)
            valid = (jj >= lo) & (jj < lo + cls_len)
            qb = qc[pl.ds(pl.multiple_of(r0 + KEYS_PER_SIDE, KEYS_PER_SIDE), ATT_QB), :]
            kw = kc[pl.ds(r0, ATT_KW), :]
            vw = vc[pl.ds(r0, ATT_KW), :]
            res = []
            for a in range(2):
                hm = head0 if a == 0 else jnp.logical_not(head0)
                qa = jnp.where(hm, qb, jnp.zeros_like(qb))
                s = lax.dot_general(qa, kw, (((1,), (1,)), ((), ())), preferred_element_type=f32)
                s = jnp.where(valid, s + bias[a], NEG_INF)
                m = jnp.max(s, axis=-1, keepdims=True)
                p = jnp.exp(s - m)
                l = jnp.sum(p, axis=-1, keepdims=True)
                o = jnp.dot(p.astype(bf16), vw, preferred_element_type=f32)
                res.append((o, m, l))
            o = jnp.where(head0, res[0][0], res[1][0])
            m = jnp.where(head0, res[0][1], res[1][1])
            l = jnp.where(head0, res[0][2], res[1][2])
            if dil == 1:
                tok = pl.ds(r0, ATT_QB)
            else:
                tok = pl.ds((r0 - cls * cls_len) * dil + cls, ATT_QB, stride=dil)
            return tok, o, m, l

        def blocks_body(i, carry, pat=pat, one_block=one_block):
            done = [one_block(ATT_NB * i + u) for u in range(ATT_NB)]
            if pat == 0:
                for tok, o, m, l in done:
                    oacc[tok, :] = o
                    macc[tok, :] = m
                    lacc[tok, :] = l
            else:
                old = [(macc[tok, :], oacc[tok, :], lacc[tok, :]) for tok, _, _, _ in done]
                for (tok, o, m, l), (m_old, o_old, l_old) in zip(done, old):
                    m_new = jnp.maximum(m_old, m)
                    a_old = jnp.exp(m_old - m_new)
                    a_new = jnp.exp(m - m_new)
                    oacc[tok, :] = a_old * o_old + a_new * o
                    lacc[tok, :] = a_old * l_old + a_new * l
                    macc[tok, :] = m_new
            return carry
        lax.fori_loop(0, seq // (ATT_QB * ATT_NB), blocks_body, 0)

    def fin_body(c, carry):
        rows = pl.ds(pl.multiple_of(c * ATT_CP, ATT_CP), ATT_CP)
        o_ref[rows, :] = (oacc[rows, :] / lacc[rows, :]).astype(bf16)
        return carry
    lax.fori_loop(0, seq // ATT_CP, fin_body, 0)


def attention(proj3, slopes, qg2, kg2):
    b, seq, _ = proj3.shape
    n_hp = ATT_WIDTH // LANES

    def col(off):
        return pl.BlockSpec((None, seq, LANES), lambda bi, h, off=off: (bi, 0, off + h),
                            pipeline_mode=pl.Buffered(1))
    gspec = pl.BlockSpec((1, LANES), lambda bi, h: (0, 0))
    return pl.pallas_call(
        functools.partial(_attn_kernel, seq=seq),
        grid=(b, n_hp),
        in_specs=[pl.BlockSpec(memory_space=pltpu.SMEM),
                  col(0), col(n_hp), col(2 * n_hp), gspec, gspec],
        out_specs=pl.BlockSpec((None, seq, LANES), lambda bi, h: (bi, 0, h)),
        out_shape=jax.ShapeDtypeStruct((b, seq, ATT_WIDTH), bf16),
        scratch_shapes=[pltpu.VMEM((seq + 2 * KEYS_PER_SIDE, LANES), bf16)] * 3
                       + [pltpu.VMEM((seq, LANES), f32)] * 3,
        compiler_params=_cparams(("arbitrary", "arbitrary")),
        name="dilated_attention",
    )(slopes, proj3, proj3, proj3, qg2, kg2)


HG_LEVELS = 6


def _shift_rows(x, j, up):
    n = x.shape[0]
    return pltpu.roll(x, (n - j) % n if up else j, axis=0)


def _hg_chunk(q, k, v, lf, st, rev, head0_rows, blockdiag):
    c = HG_CHUNK
    row = lax.broadcasted_iota(i32, (c, LANES), 0)
    cum = lf
    for lv in range(HG_LEVELS):
        j = 1 << lv
        sh = _shift_rows(cum, j, up=rev)
        ok = (row < c - j) if rev else (row >= j)
        cum = cum + jnp.where(ok, sh, 0.0)

    trow = lax.broadcasted_iota(i32, (2 * c, c), 0) & (c - 1)
    scol = lax.broadcasted_iota(i32, (2 * c, c), 1)

    def pair_scores(a_mat, b_mat, lv):
        a2 = jnp.concatenate([jnp.where(head0_rows, a_mat, 0.0), jnp.where(head0_rows, 0.0, a_mat)], axis=0)
        sc = lax.dot_general(a2.astype(bf16), b_mat.astype(bf16), (((1,), (1,)), ((), ())),
                             preferred_element_type=f32)
        same = lax.shift_right_logical(trow, lv) == lax.shift_right_logical(scol, lv)
        return jnp.where(same, sc, 0.0)

    scores = pair_scores(q, k, 0)
    edge = cum
    for lv in range(1, HG_LEVELS + 1):
        h = 1 << (lv - 1)
        if h > 1:
            hh = h >> 1
            sh = _shift_rows(edge, hh, up=not rev)
            take = ((row & hh) == 0) != rev
            edge = jnp.where(take, sh, edge)
        far = ((row & h) != 0) != rev
        edge_prev = _shift_rows(edge, h, up=rev)
        a_mat = q * jnp.exp(jnp.where(far, cum - edge_prev, NEG_INF))
        b_mat = k * jnp.exp(jnp.where(far, NEG_INF, edge - cum))
        scores = scores + pair_scores(a_mat, b_mat, lv)

    hh = c >> 1
    sh = _shift_rows(edge, hh, up=not rev)
    total = jnp.where(((row & hh) == 0) != rev, sh, edge)

    vb = v.astype(bf16)
    scb = scores.astype(bf16)
    intra0 = jnp.dot(scb[0:c], vb, preferred_element_type=f32)
    intra1 = jnp.dot(scb[c:2 * c], vb, preferred_element_type=f32)
    intra = jnp.where(head0_rows, intra0, intra1)
    qe = (q * jnp.exp(cum)).astype(bf16)
    inter = lax.dot_general(qe, st.astype(bf16), (((1,), (1,)), ((), ())), preferred_element_type=f32)
    kd = (k * jnp.exp(total - cum)).astype(bf16)
    upd = lax.dot_general(vb, kd, (((0,), (0,)), ((), ())), preferred_element_type=f32)
    st_new = st * jnp.exp(total[0:1, :]) + jnp.where(blockdiag, upd, 0.0)
    return inter + intra, st_new


def _hgrn_kernel(lb_ref, q_ref, zf_ref, zb_ref, i_ref, g_ref, hn_ref, o_ref, of_scr, ob_scr, *, seq):
    c = HG_CHUNK
    head0_rows = lax.broadcasted_iota(i32, (c, LANES), 1) < HEAD_DIM
    r2 = lax.broadcasted_iota(i32, (LANES, LANES), 0) < HEAD_DIM
    c2 = lax.broadcasted_iota(i32, (LANES, LANES), 1) < HEAD_DIM
    blockdiag = r2 == c2

    lbs = []
    for d in range(2):
        a0 = lb_ref[d, 0:1, :]
        a1 = lb_ref[d, 1:2, :]
        mx = jnp.maximum(a0, a1)
        e0 = jnp.exp(a0 - mx)
        e1 = jnp.exp(a1 - mx)
        lbs.append(e0 / (e0 + e1))

    n_chunks = seq // c

    def gates(z, lb):
        f = lb + (1.0 - lb) * jax.nn.sigmoid(z)
        return jnp.log(f), 1.0 - f

    def scan_body(ci, states):
        st_f, st_b = states
        rows_f = pl.ds(pl.multiple_of(ci * c, c), c)
        rows_b = pl.ds(pl.multiple_of((n_chunks - 1 - ci) * c, c), c)
        lf_f, k_f = gates(zf_ref[rows_f, :], lbs[0])
        lf_b, k_b = gates(zb_ref[rows_b, :], lbs[1])
        o_f, st_f = _hg_chunk(q_ref[rows_f, :], k_f, i_ref[rows_f, :], lf_f, st_f, False, head0_rows, blockdiag)
        o_b, st_b = _hg_chunk(q_ref[rows_b, :], k_b, i_ref[rows_b, :], lf_b, st_b, True, head0_rows, blockdiag)
        of_scr[rows_f, :] = o_f
        ob_scr[rows_b, :] = o_b
        return st_f, st_b
    zero_state = jnp.zeros((LANES, LANES), f32)
    lax.fori_loop(0, n_chunks, scan_body, (zero_state, zero_state))

    hn = hn_ref[...]
    same_head = blockdiag.astype(bf16)
    fin = 4 * c

    def fin_body(ci, carry):
        rows = pl.ds(pl.multiple_of(ci * fin, fin), fin)
        y = _pair_rmsnorm(of_scr[rows, :] + ob_scr[rows, :], hn, same_head)
        o_ref[rows, :] = (y * jax.nn.silu(g_ref[rows, :])).astype(bf16)
        return carry
    lax.fori_loop(0, seq // fin, fin_body, 0)


def hgrn(proj3, lower_bounds, hn2):
    b, seq, _ = proj3.shape
    n_hp = HG_WIDTH // LANES
    base = 3 * ATT_WIDTH // LANES

    def col(k):
        return pl.BlockSpec((None, seq, LANES), lambda bi, h, k=k: (bi, 0, base + k * n_hp + h),
                            pipeline_mode=pl.Buffered(1))
    return pl.pallas_call(
        functools.partial(_hgrn_kernel, seq=seq),
        grid=(b, n_hp),
        in_specs=[pl.BlockSpec((2, 2, LANES), lambda bi, h: (0, 0, h)),
                  col(0), col(1), col(2), col(3), col(4),
                  pl.BlockSpec((1, LANES), lambda bi, h: (0, 0))],
        out_specs=pl.BlockSpec((None, seq, LANES), lambda bi, h: (bi, 0, h)),
        out_shape=jax.ShapeDtypeStruct((b, seq, HG_WIDTH), bf16),
        scratch_shapes=[pltpu.VMEM((seq, LANES), f32)] * 2,
        compiler_params=_cparams(("arbitrary", "arbitrary")),
        name="hgrn2_bidir",
    )(lower_bounds, proj3, proj3, proj3, proj3, proj3, hn2)


OUT_TM = 512


def _outproj_kernel(x_ref, att_ref, hg_ref, w_ref, g_ref, h_ref, xn_ref):
    h = (x_ref[...]
         + jnp.dot(att_ref[...], w_ref[0:ATT_WIDTH, :], preferred_element_type=f32)
         + jnp.dot(hg_ref[...], w_ref[ATT_WIDTH:, :], preferred_element_type=f32))
    h_ref[...] = h
    ms = jnp.mean(h * h, axis=-1, keepdims=True)
    xn_ref[...] = (h * lax.rsqrt(ms + NORM_EPS)) * g_ref[...]


def out_proj(x2d, att2d, hg2d, w_bf16, gain):
    n = x2d.shape[0]
    row = lambda w: pl.BlockSpec((OUT_TM, w), lambda i: (i, 0))
    return pl.pallas_call(
        _outproj_kernel,
        grid=(n // OUT_TM,),
        in_specs=[row(D_MODEL), row(ATT_WIDTH), row(HG_WIDTH),
                  pl.BlockSpec((D_MODEL, D_MODEL), lambda i: (0, 0)),
                  pl.BlockSpec((1, D_MODEL), lambda i: (0, 0))],
        out_specs=[row(D_MODEL), row(D_MODEL)],
        out_shape=[jax.ShapeDtypeStruct((n, D_MODEL), f32)] * 2,
        compiler_params=_cparams(("arbitrary",)),
        name="out_proj",
    )(x2d, att2d, hg2d, w_bf16, gain)


RT_T = 256
PEER_TT = 128
PEER_J = PEER_HEADS * PEER_TOPK
BIG = 1.0e9


def _top16(s, posf):
    vals, poss = [], []
    for _ in range(PEER_TOPK):
        m = jnp.max(s, axis=0, keepdims=True)
        pos = jnp.min(jnp.where(s == m, posf, BIG), axis=0, keepdims=True)
        vals.append(m)
        poss.append(pos)
        s = jnp.where(posf == pos, NEG_INF, s)
    return vals, poss


def _stack16(rows):
    r16 = lax.broadcasted_iota(i32, (PEER_TOPK, rows[0].shape[1]), 0)
    out = jnp.zeros((PEER_TOPK, rows[0].shape[1]), f32)
    for kk, rw in enumerate(rows):
        out = jnp.where(r16 == kk, rw, out)
    return out


def _route_head(s0, s1):
    cdim = s0.shape[1]
    keypos = lax.broadcasted_iota(i32, (PEER_N_KEYS, cdim), 0).astype(f32)
    v0, p0 = _top16(s0, keypos)
    v1, p1 = _top16(s1, keypos)
    sv0, sp0, sv1, sp1 = _stack16(v0), _stack16(p0), _stack16(v1), _stack16(p1)

    r8 = lax.broadcasted_iota(i32, (SUBLANES, cdim), 0)
    groups_v = [v0[0] + sv1]
    groups_e = [p0[0] * PEER_N_KEYS + sp1]
    groups_p = [lax.broadcasted_iota(i32, (PEER_TOPK, cdim), 0).astype(f32)]
    for i in range(1, 8):
        nj = PEER_TOPK // (i + 1)
        ok = r8 < nj
        groups_v.append(jnp.where(ok, v0[i] + sv1[0:SUBLANES], NEG_INF))
        groups_e.append(p0[i] * PEER_N_KEYS + sp1[0:SUBLANES])
        groups_p.append((r8 + i * PEER_TOPK).astype(f32))
    groups_v.append(sv0[SUBLANES:] + v1[0])
    groups_e.append(sp0[SUBLANES:] * PEER_N_KEYS + p1[0])
    groups_p.append(((r8 + SUBLANES) * PEER_TOPK).astype(f32))
    cand = jnp.concatenate(groups_v, axis=0)
    cexp = jnp.concatenate(groups_e, axis=0)
    cpos = jnp.concatenate(groups_p, axis=0)

    vals, exps = [], []
    for _ in range(PEER_TOPK):
        m = jnp.max(cand, axis=0, keepdims=True)
        pos = jnp.min(jnp.where(cand == m, cpos, BIG), axis=0, keepdims=True)
        sel = cpos == pos
        exps.append(jnp.max(jnp.where(sel, cexp, -1.0), axis=0, keepdims=True))
        vals.append(m)
        cand = jnp.where(sel, NEG_INF, cand)
    sv = _stack16(vals)
    ex = jnp.exp(sv - vals[0])
    gate = ex / jnp.sum(ex, axis=0, keepdims=True)
    return _stack16(exps), gate


def _route_kernel(xn_ref, wqt_ref, keys_ref, idx_ref, gate_ref, qt_scr, eid_scr):
    xb = xn_ref[...].astype(bf16)
    qt_scr[...] = lax.dot_general(wqt_ref[...], xb, (((1,), (1,)), ((), ())),
                                  preferred_element_type=f32)

    def head_body(h, carry):
        for ci in range(RT_T // PEER_TT):
            cols = slice(ci * PEER_TT, (ci + 1) * PEER_TT)
            halves = []
            for p in range(2):
                r0 = pl.multiple_of((2 * h + p) * PEER_HALF, PEER_HALF)
                qh = qt_scr[pl.ds(r0, PEER_HALF), cols].astype(bf16)
                halves.append(jnp.dot(keys_ref[2 * h + p], qh, preferred_element_type=f32))
            eid, gate = _route_head(halves[0], halves[1])
            rows = pl.ds(pl.multiple_of(h * PEER_TOPK, PEER_TOPK), PEER_TOPK)
            eid_scr[ci, rows, :] = eid
            gate_ref[ci, rows, :] = gate
        return carry
    lax.fori_loop(0, PEER_HEADS, head_body, 0)
    for ci in range(RT_T // PEER_TT):
        idx_ref[ci] = (eid_scr[ci].T * float(SUBLANES // 2)).astype(i32)


def route(xn2d, wqt_bf16, keys_bf16):
    n = xn2d.shape[0]
    n_tiles = n // PEER_TT
    per = RT_T // PEER_TT
    ospec = pl.BlockSpec((per, PEER_J, PEER_TT), lambda i: (i, 0, 0))
    return pl.pallas_call(
        _route_kernel,
        grid=(n // RT_T,),
        in_specs=[pl.BlockSpec((RT_T, D_MODEL), lambda i: (i, 0)),
                  pl.BlockSpec((2 * PEER_HEADS * PEER_HALF, D_MODEL), lambda i: (0, 0)),
                  pl.BlockSpec((2 * PEER_HEADS, PEER_N_KEYS, PEER_HALF), lambda i: (0, 0, 0))],
        out_specs=[pl.BlockSpec((per, PEER_TT, PEER_J), lambda i: (i, 0, 0)), ospec],
        out_shape=[jax.ShapeDtypeStruct((n_tiles, PEER_TT, PEER_J), i32),
                   jax.ShapeDtypeStruct((n_tiles, PEER_J, PEER_TT), f32)],
        scratch_shapes=[pltpu.VMEM((2 * PEER_HEADS * PEER_HALF, RT_T), f32),
                        pltpu.VMEM((per, PEER_J, PEER_TT), f32)],
        compiler_params=_cparams(("arbitrary",)),
        name="peer_route",
    )(xn2d, wqt_bf16, keys_bf16)


PEER_FLAT = PEER_TT * PEER_J
PACK_ROWS = SUBLANES // 2
PEER_NACC = 4


def _resident_tile_offset(step, n_steps, idx_hbm, idx_smem, sem):
    def copy(tile, slot):
        dst = idx_smem.at[pl.ds(pl.multiple_of(slot * PEER_FLAT, PEER_FLAT), PEER_FLAT)]
        return pltpu.make_async_copy(idx_hbm.at[tile], dst, sem.at[slot])

    slot = step % 2

    @pl.when(step == 0)
    def _():
        copy(0, 0).start()

    copy(step, slot).wait()

    @pl.when(step + 1 < n_steps)
    def _():
        copy(step + 1, 1 - slot).start()
    return slot * PEER_FLAT


def _sublane_fold(x, y, s, sub):
    clear = (sub & s) == 0
    a = jnp.where(clear, x, y)
    b = jnp.where(clear, y, x)
    if 2 * s == SUBLANES:
        partner = pltpu.roll(b, s, axis=0)
    else:
        partner = jnp.where(clear, pltpu.roll(b, SUBLANES - s, axis=0), pltpu.roll(b, s, axis=0))
    return a + partner


def _expert_row(tab_ref, row4):
    words = tab_ref[pl.ds(pl.multiple_of(row4, PACK_ROWS), PACK_ROWS), :]
    return pltpu.bitcast(words, bf16).astype(f32)


_BITREV3 = (0, 4, 2, 6, 1, 5, 3, 7)
PEER_GROUPS = PEER_J // SUBLANES


def _peer_u_kernel(idx_hbm, tab_ref, x_ref, gate_ref, w_ref, idx_smem, sem, h_scr):
    off = _resident_tile_offset(pl.program_id(0), pl.num_programs(0), idx_hbm, idx_smem, sem)
    lane = lax.broadcasted_iota(i32, (SUBLANES, PEER_TT), 1)
    sub = lax.broadcasted_iota(i32, (SUBLANES, LANES), 0)
    h_scr[...] = jnp.zeros((PEER_J, PEER_TT), f32)

    def retire(t_done, partial):
        for g in range(PEER_GROUPS):
            col = jnp.sum(partial[g], axis=-1, keepdims=True)
            rows = slice(g * SUBLANES, (g + 1) * SUBLANES)
            h_scr[rows, :] = jnp.where(lane == t_done, col, h_scr[rows, :])

    def tok_body(t, prev):
        retire(t - 1, prev)
        xt = x_ref[t]
        base = off + t * PEER_J
        partial = []
        for g in range(PEER_GROUPS):
            p = [_expert_row(tab_ref, idx_smem[base + g * SUBLANES + jj]) * xt for jj in _BITREV3]
            q = [_sublane_fold(p[2 * i], p[2 * i + 1], 4, sub) for i in range(4)]
            r = [_sublane_fold(q[2 * i], q[2 * i + 1], 2, sub) for i in range(2)]
            partial.append(_sublane_fold(r[0], r[1], 1, sub))
        return tuple(partial)
    zero = jnp.zeros((SUBLANES, LANES), f32)
    last = lax.fori_loop(0, PEER_TT, tok_body, (zero,) * PEER_GROUPS)
    retire(PEER_TT - 1, last)
    w_ref[...] = gate_ref[...] * jax.nn.gelu(h_scr[...])


def peer_u_pass(idx2, gate3, tab3, xn3):
    n_tiles = idx2.shape[0]
    tspec = pl.BlockSpec((None, PEER_J, PEER_TT), lambda i: (i, 0, 0))
    return pl.pallas_call(
        _peer_u_kernel,
        grid=(n_tiles,),
        in_specs=[pl.BlockSpec(memory_space=pl.ANY),
                  pl.BlockSpec(memory_space=pltpu.VMEM),
                  pl.BlockSpec((PEER_TT, SUBLANES, LANES), lambda i: (i, 0, 0)),
                  tspec],
        out_specs=tspec,
        out_shape=jax.ShapeDtypeStruct((n_tiles, PEER_J, PEER_TT), f32),
        scratch_shapes=[pltpu.SMEM((2 * PEER_FLAT,), i32),
                        pltpu.SemaphoreType.DMA((2,)),
                        pltpu.VMEM((PEER_J, PEER_TT), f32)],
        compiler_params=_cparams(("arbitrary",)),
        name="peer_u",
    )(idx2, tab3, xn3, gate3)


def _peer_v_kernel(idx_hbm, tab_ref, w_ref, h_ref, y_ref, idx_smem, sem):
    off = _resident_tile_offset(pl.program_id(0), pl.num_programs(0), idx_hbm, idx_smem, sem)
    lane = lax.broadcasted_iota(i32, (SUBLANES, PEER_TT), 1)

    def lane_splat(t):
        out = []
        for g in range(PEER_GROUPS):
            wg = w_ref[g * SUBLANES:(g + 1) * SUBLANES, :]
            col = jnp.sum(jnp.where(lane == t, wg, 0.0), axis=-1, keepdims=True)
            out.append(jnp.broadcast_to(col, (SUBLANES, LANES)))
        return tuple(out)

    def tok_body(t, wsplat):
        nxt = lane_splat(t + 1)
        base = off + t * PEER_J
        zero = jnp.zeros((SUBLANES, LANES), f32)
        accs = [h_ref[t]] + [zero] * (PEER_NACC - 1)
        for g in range(PEER_GROUPS):
            p = []
            for i in range(SUBLANES):
                row = _expert_row(tab_ref, idx_smem[base + g * SUBLANES + i])
                p.append(jnp.broadcast_to(wsplat[g][i:i + 1, :], (SUBLANES, LANES)) * row)
            s = ((p[0] + p[1]) + (p[2] + p[3])) + ((p[4] + p[5]) + (p[6] + p[7]))
            accs[g % PEER_NACC] = accs[g % PEER_NACC] + s
        y_ref[t] = (accs[0] + accs[1]) + (accs[2] + accs[3])
        return nxt
    lax.fori_loop(0, PEER_TT, tok_body, lane_splat(0))


def peer_v_pass(idx2, w3, tab3, h3):
    n_tiles = idx2.shape[0]
    xspec = pl.BlockSpec((PEER_TT, SUBLANES, LANES), lambda i: (i, 0, 0))
    return pl.pallas_call(
        _peer_v_kernel,
        grid=(n_tiles,),
        in_specs=[pl.BlockSpec(memory_space=pl.ANY),
                  pl.BlockSpec(memory_space=pltpu.VMEM),
                  pl.BlockSpec((None, PEER_J, PEER_TT), lambda i: (i, 0, 0)),
                  xspec],
        out_specs=xspec,
        out_shape=jax.ShapeDtypeStruct(h3.shape, f32),
        scratch_shapes=[pltpu.SMEM((2 * PEER_FLAT,), i32),
                        pltpu.SemaphoreType.DMA((2,))],
        compiler_params=_cparams(("arbitrary",)),
        name="peer_v",
    )(idx2, tab3, w3, h3)


def _trunk(x, p):
    b, seq, d = x.shape
    n = b * seq
    x2d = x.reshape(n, d)
    proj3 = in_proj(x2d, p["norm1"], p["w_in"]).reshape(b, seq, N_IN_COLS)
    att = attention(proj3, p["slopes"], p["q_gain"], p["k_gain"])
    hg = hgrn(proj3, p["lower_bounds"], p["hg_gain"])
    h1, xn2 = out_proj(x2d, att.reshape(n, ATT_WIDTH), hg.reshape(n, HG_WIDTH), p["w_out"], p["norm2"])
    idx3, gate3 = route(xn2, p["wq_t"], p["keys"])
    idx2 = idx3.reshape(-1, PEER_FLAT)
    w3 = peer_u_pass(idx2, gate3, p["u_tab"], xn2.reshape(n, SUBLANES, LANES))
    y3 = peer_v_pass(idx2, w3, p["v_tab"], h1.reshape(n, SUBLANES, LANES))
    return y3.reshape(b, seq, d)


def _pack_table(tab):
    n_exp = tab.shape[0]
    pairs = tab.astype(bf16).reshape(n_exp, SUBLANES // 2, 2, LANES).transpose(0, 1, 3, 2)
    return lax.bitcast_convert_type(pairs, jnp.uint32).reshape(n_exp * (SUBLANES // 2), LANES)


def kernel(x_prompt, x_sample, norm1, w_in, q_norm, k_norm, lower_bounds, hg_norm, w_out, norm2,
           peer_wq, peer_keys, peer_u, peer_v):
    n_heads = ATT_WIDTH // HEAD_DIM
    p = {
        "norm1": norm1[0].reshape(1, D_MODEL),
        "w_in": w_in[0].astype(bf16),
        "q_gain": jnp.tile(q_norm[0], 2).reshape(1, LANES),
        "k_gain": jnp.tile(k_norm[0], 2).reshape(1, LANES),
        "hg_gain": jnp.tile(hg_norm[0], 2).reshape(1, LANES),
        "slopes": 2.0 ** (-8.0 * jnp.arange(1, n_heads + 1, dtype=f32) / n_heads),
        "lower_bounds": lower_bounds.astype(f32),
        "w_out": w_out[0].astype(bf16),
        "norm2": norm2[0].reshape(1, D_MODEL),
        "wq_t": peer_wq[0].T.astype(bf16),
        "keys": peer_keys[0].reshape(2 * PEER_HEADS, PEER_N_KEYS, PEER_HALF).astype(bf16),
        "u_tab": _pack_table(peer_u[0]),
        "v_tab": _pack_table(peer_v[0]),
    }
    return (_trunk(x_prompt, p), _trunk(x_sample, p))
```

```python
import functools

import jax
import jax.numpy as jnp
from jax import lax
from jax.experimental import pallas as pl
from jax.experimental.pallas import tpu as pltpu

f32 = jnp.float32
bf16 = jnp.bfloat16
i32 = jnp.int32

D_MODEL = 1024
ATT_WIDTH = 512
HG_WIDTH = 512
HEAD_DIM = 64
N_IN_COLS = 3 * ATT_WIDTH + 5 * HG_WIDTH
DILATIONS = (1, 4, 16)
KEYS_PER_SIDE = 64
HG_CHUNK = 64
PEER_HEADS = 8
PEER_N_KEYS = 128
PEER_TOPK = 16
PEER_HALF = 128
NORM_EPS = 1e-6

LANES = 128
SUBLANES = 8
VMEM_LIMIT = 56 * 1024 * 1024
NEG_INF = float("-inf")


def _cparams(sem=None, flags=None):
    return pltpu.CompilerParams(dimension_semantics=sem, vmem_limit_bytes=VMEM_LIMIT, flags=flags)


IN_TM = 512


def _inproj_kernel(x_ref, g_ref, w_ref, o_ref):
    x = x_ref[...]
    ms = jnp.mean(x * x, axis=-1, keepdims=True)
    xn = ((x * lax.rsqrt(ms + NORM_EPS)) * g_ref[...]).astype(bf16)
    tn = 1024
    for c in range(N_IN_COLS // tn):
        o_ref[:, c * tn:(c + 1) * tn] = jnp.dot(xn, w_ref[:, c * tn:(c + 1) * tn],
                                                preferred_element_type=f32)


def in_proj(x2d, gain, w_bf16):
    n = x2d.shape[0]
    return pl.pallas_call(
        _inproj_kernel,
        grid=(n // IN_TM,),
        in_specs=[pl.BlockSpec((IN_TM, D_MODEL), lambda i: (i, 0)),
                  pl.BlockSpec((1, D_MODEL), lambda i: (0, 0)),
                  pl.BlockSpec((D_MODEL, N_IN_COLS), lambda i: (0, 0))],
        out_specs=pl.BlockSpec((IN_TM, N_IN_COLS), lambda i: (i, 0)),
        out_shape=jax.ShapeDtypeStruct((n, N_IN_COLS), f32),
        compiler_params=_cparams(("arbitrary",)),
        name="in_proj",
    )(x2d, gain, w_bf16)


def _pair_rmsnorm(x, gain, same_head):
    x2 = x * x
    hi = x2.astype(bf16)
    lo = (x2 - hi.astype(f32)).astype(bf16)
    ss = (jnp.dot(hi, same_head, preferred_element_type=f32)
          + jnp.dot(lo, same_head, preferred_element_type=f32))
    return (x * lax.rsqrt(ss * (1.0 / HEAD_DIM) + NORM_EPS)) * gain


def _same_head_matrix():
    r = lax.broadcasted_iota(i32, (LANES, LANES), 0) < HEAD_DIM
    c = lax.broadcasted_iota(i32, (LANES, LANES), 1) < HEAD_DIM
    return (r == c).astype(bf16)


ATT_QB = 128
ATT_KW = ATT_QB + 2 * KEYS_PER_SIDE
ATT_CP = 256
ATT_CPU = 2
ATT_NB = 8


def _attn_kernel(slopes_ref, q_ref, k_ref, v_ref, qg_ref, kg_ref, o_ref,
                 qc, kc, vc, oacc, macc, lacc, *, seq):
    hp = pl.program_id(1)
    lane = lax.broadcasted_iota(i32, (1, LANES), 1)
    head0 = lane < HEAD_DIM
    same_head = _same_head_matrix()
    qg = qg_ref[...]
    kg = kg_ref[...]
    zpad = jnp.zeros((KEYS_PER_SIDE, LANES), bf16)
    for ref in (qc, kc, vc):
        ref[0:KEYS_PER_SIDE, :] = zpad
        ref[seq + KEYS_PER_SIDE:seq + 2 * KEYS_PER_SIDE, :] = zpad
    slope = (slopes_ref[2 * hp], slopes_ref[2 * hp + 1])

    iq = lax.broadcasted_iota(i32, (ATT_QB, ATT_KW), 0)
    jj = lax.broadcasted_iota(i32, (ATT_QB, ATT_KW), 1)
    rel = jnp.abs(iq + KEYS_PER_SIDE - jj)
    band = rel <= KEYS_PER_SIDE

    for pat, dil in enumerate(DILATIONS):
        cls_len = seq // dil
        shift = cls_len.bit_length() - 1
        cp = min(ATT_CP, cls_len)
        n_cp = cls_len // cp

        def copy_body(i, carry, dil=dil, cp=cp, n_cp=n_cp):
            for u in range(ATT_CPU):
                chunk = ATT_CPU * i + u
                if dil == 1:
                    src = pl.ds(pl.multiple_of(chunk * cp, cp), cp)
                else:
                    r = lax.shift_right_logical(chunk, n_cp.bit_length() - 1)
                    c = chunk & (n_cp - 1)
                    src = pl.ds(r + dil * cp * c, cp, stride=dil)
                dst = pl.ds(pl.multiple_of(KEYS_PER_SIDE + chunk * cp, KEYS_PER_SIDE), cp)
                qn = _pair_rmsnorm(q_ref[src, :], qg, same_head) * (HEAD_DIM ** -0.5)
                qc[dst, :] = qn.astype(bf16)
                kc[dst, :] = _pair_rmsnorm(k_ref[src, :], kg, same_head).astype(bf16)
                vc[dst, :] = v_ref[src, :].astype(bf16)
            return carry
        lax.fori_loop(0, seq // (cp * ATT_CPU), copy_body, 0)

        relf = (dil * rel).astype(f32)
        bias = [jnp.where(band, -slope[a] * relf, NEG_INF) for a in range(2)]

        def one_block(i, dil=dil, cls_len=cls_len, shift=shift, bias=bias):
            r0 = pl.multiple_of(i * ATT_QB, ATT_QB)
            cls = lax.shift_right_logical(r0, shift)
            lo = cls * cls_len - (r0 - KEYS_PER_SIDE)
            valid = (jj >= lo) & (jj < lo + cls_len)
            qb = qc[pl.ds(pl.multiple_of(r0 + KEYS_PER_SIDE, KEYS_PER_SIDE), ATT_QB), :]
            kw = kc[pl.ds(r0, ATT_KW), :]
            vw = vc[pl.ds(r0, ATT_KW), :]
            res = []
            for a in range(2):
                hm = head0 if a == 0 else jnp.logical_not(head0)
                qa = jnp.where(hm, qb, jnp.zeros_like(qb))
                s = lax.dot_general(qa, kw, (((1,), (1,)), ((), ())), preferred_element_type=f32)
                s = jnp.where(valid, s + bias[a], NEG_INF)
                m = jnp.max(s, axis=-1, keepdims=True)
                p = jnp.exp(s - m)
                l = jnp.sum(p, axis=-1, keepdims=True)
                o = jnp.dot(p.astype(bf16), vw, preferred_element_type=f32)
                res.append((o, m, l))
            o = jnp.where(head0, res[0][0], res[1][0])
            m = jnp.where(head0, res[0][1], res[1][1])
            l = jnp.where(head0, res[0][2], res[1][2])
            if dil == 1:
                tok = pl.ds(r0, ATT_QB)
            else:
                tok = pl.ds((r0 - cls * cls_len) * dil + cls, ATT_QB, stride=dil)
            return tok, o, m, l

        def blocks_body(i, carry, pat=pat, one_block=one_block):
            done = [one_block(ATT_NB * i + u) for u in range(ATT_NB)]
            if pat == 0:
                for tok, o, m, l in done:
                    oacc[tok, :] = o
                    macc[tok, :] = m
                    lacc[tok, :] = l
            else:
                old = [(macc[tok, :], oacc[tok, :], lacc[tok, :]) for tok, _, _, _ in done]
                for (tok, o, m, l), (m_old, o_old, l_old) in zip(done, old):
                    m_new = jnp.maximum(m_old, m)
                    a_old = jnp.exp(m_old - m_new)
                    a_new = jnp.exp(m - m_new)
                    oacc[tok, :] = a_old * o_old + a_new * o
                    lacc[tok, :] = a_old * l_old + a_new * l
                    macc[tok, :] = m_new
            return carry
        lax.fori_loop(0, seq // (ATT_QB * ATT_NB), blocks_body, 0)

    def fin_body(c, carry):
        rows = pl.ds(pl.multiple_of(c * ATT_CP, ATT_CP), ATT_CP)
        o_ref[rows, :] = (oacc[rows, :] / lacc[rows, :]).astype(bf16)
        return carry
    lax.fori_loop(0, seq // ATT_CP, fin_body, 0)


def attention(proj3, slopes, qg2, kg2):
    b, seq, _ = proj3.shape
    n_hp = ATT_WIDTH // LANES

    def col(off):
        return pl.BlockSpec((None, seq, LANES), lambda bi, h, off=off: (bi, 0, off + h),
                            pipeline_mode=pl.Buffered(1))
    gspec = pl.BlockSpec((1, LANES), lambda bi, h: (0, 0))
    return pl.pallas_call(
        functools.partial(_attn_kernel, seq=seq),
        grid=(b, n_hp),
        in_specs=[pl.BlockSpec(memory_space=pltpu.SMEM),
                  col(0), col(n_hp), col(2 * n_hp), gspec, gspec],
        out_specs=pl.BlockSpec((None, seq, LANES), lambda bi, h: (bi, 0, h)),
        out_shape=jax.ShapeDtypeStruct((b, seq, ATT_WIDTH), bf16),
        scratch_shapes=[pltpu.VMEM((seq + 2 * KEYS_PER_SIDE, LANES), bf16)] * 3
                       + [pltpu.VMEM((seq, LANES), f32)] * 3,
        compiler_params=_cparams(("arbitrary", "arbitrary")),
        name="dilated_attention",
    )(slopes, proj3, proj3, proj3, qg2, kg2)


HG_LEVELS = 6
HG_PER_ITER = 4


def _shift_rows(x, j, up):
    n = x.shape[0]
    return pltpu.roll(x, (n - j) % n if up else j, axis=0)


def _hg_chunk(q, k, v, lf, rev, head0_rows, blockdiag):
    c = HG_CHUNK
    row = lax.broadcasted_iota(i32, (c, LANES), 0)
    cum = lf
    for lv in range(HG_LEVELS):
        j = 1 << lv
        sh = _shift_rows(cum, j, up=rev)
        ok = (row < c - j) if rev else (row >= j)
        cum = cum + jnp.where(ok, sh, 0.0)

    trow = lax.broadcasted_iota(i32, (2 * c, c), 0) & (c - 1)
    scol = lax.broadcasted_iota(i32, (2 * c, c), 1)
    differ = trow ^ scol
    q_h0 = jnp.where(head0_rows, q, 0.0)
    q_h1 = q - q_h0

    def pair_scores(a_decay, b_mat):
        a2 = jnp.concatenate([q_h0 * a_decay, q_h1 * a_decay], axis=0) if a_decay is not None else \
            jnp.concatenate([q_h0, q_h1], axis=0)
        return lax.dot_general(a2.astype(bf16), b_mat.astype(bf16), (((1,), (1,)), ((), ())),
                               preferred_element_type=f32)

    scores = pair_scores(None, k)
    edge = cum
    for lv in range(1, HG_LEVELS + 1):
        h = 1 << (lv - 1)
        if h > 1:
            hh = h >> 1
            sh = _shift_rows(edge, hh, up=not rev)
            take = ((row & hh) == 0) != rev
            edge = jnp.where(take, sh, edge)
        far = ((row & h) != 0) != rev
        edge_prev = _shift_rows(edge, h, up=rev)
        a_decay = jnp.exp(jnp.where(far, cum - edge_prev, NEG_INF))
        b_mat = k * jnp.exp(jnp.where(far, NEG_INF, edge - cum))
        scores = jnp.where(differ < h, scores, pair_scores(a_decay, b_mat))

    hh = c >> 1
    sh = _shift_rows(edge, hh, up=not rev)
    total = jnp.where(((row & hh) == 0) != rev, sh, edge)

    vb = v.astype(bf16)
    scb = scores.astype(bf16)
    intra0 = jnp.dot(scb[0:c], vb, preferred_element_type=f32)
    intra1 = jnp.dot(scb[c:2 * c], vb, preferred_element_type=f32)
    intra = jnp.where(head0_rows, intra0, intra1)
    qe = (q * jnp.exp(cum)).astype(bf16)
    kd = (k * jnp.exp(total - cum)).astype(bf16)
    upd = lax.dot_general(vb, kd, (((0,), (0,)), ((), ())), preferred_element_type=f32)
    return intra, qe, jnp.where(blockdiag, upd, 0.0), jnp.exp(total[0:1, :])


def _hg_state_step(st, intra, qe, upd, decay):
    inter = lax.dot_general(qe, st.astype(bf16), (((1,), (1,)), ((), ())), preferred_element_type=f32)
    return inter + intra, st * decay + upd


def _hgrn_kernel(lb_ref, q_ref, zf_ref, zb_ref, i_ref, g_ref, hn_ref, o_ref, of_scr, ob_scr, *, seq):
    c = HG_CHUNK
    head0_rows = lax.broadcasted_iota(i32, (c, LANES), 1) < HEAD_DIM
    r2 = lax.broadcasted_iota(i32, (LANES, LANES), 0) < HEAD_DIM
    c2 = lax.broadcasted_iota(i32, (LANES, LANES), 1) < HEAD_DIM
    blockdiag = r2 == c2

    lbs = []
    for d in range(2):
        a0 = lb_ref[d, 0:1, :]
        a1 = lb_ref[d, 1:2, :]
        mx = jnp.maximum(a0, a1)
        e0 = jnp.exp(a0 - mx)
        e1 = jnp.exp(a1 - mx)
        lbs.append(e0 / (e0 + e1))

    n_chunks = seq // c

    def gates(z, lb):
        f = lb + (1.0 - lb) * jax.nn.sigmoid(z)
        return jnp.log(f), 1.0 - f

    def scan_body(it, states):
        st_f, st_b = states
        work = []
        for u in range(HG_PER_ITER):
            ci = it * HG_PER_ITER + u
            rows_f = pl.ds(pl.multiple_of(ci * c, c), c)
            rows_b = pl.ds(pl.multiple_of((n_chunks - 1 - ci) * c, c), c)
            lf_f, k_f = gates(zf_ref[rows_f, :], lbs[0])
            lf_b, k_b = gates(zb_ref[rows_b, :], lbs[1])
            work.append((rows_f, _hg_chunk(q_ref[rows_f, :], k_f, i_ref[rows_f, :], lf_f, False,
                                           head0_rows, blockdiag),
                         rows_b, _hg_chunk(q_ref[rows_b, :], k_b, i_ref[rows_b, :], lf_b, True,
                                           head0_rows, blockdiag)))
        for rows_f, part_f, rows_b, part_b in work:
            o_f, st_f = _hg_state_step(st_f, *part_f)
            o_b, st_b = _hg_state_step(st_b, *part_b)
            of_scr[rows_f, :] = o_f
            ob_scr[rows_b, :] = o_b
        return st_f, st_b
    zero_state = jnp.zeros((LANES, LANES), f32)
    lax.fori_loop(0, n_chunks // HG_PER_ITER, scan_body, (zero_state, zero_state))

    hn = hn_ref[...]
    same_head = blockdiag.astype(bf16)
    fin = 4 * c

    def fin_body(ci, carry):
        rows = pl.ds(pl.multiple_of(ci * fin, fin), fin)
        y = _pair_rmsnorm(of_scr[rows, :] + ob_scr[rows, :], hn, same_head)
        o_ref[rows, :] = (y * jax.nn.silu(g_ref[rows, :])).astype(bf16)
        return carry
    lax.fori_loop(0, seq // fin, fin_body, 0)


def hgrn(proj3, lower_bounds, hn2):
    b, seq, _ = proj3.shape
    n_hp = HG_WIDTH // LANES
    base = 3 * ATT_WIDTH // LANES

    def col(k):
        return pl.BlockSpec((None, seq, LANES), lambda bi, h, k=k: (bi, 0, base + k * n_hp + h),
                            pipeline_mode=pl.Buffered(1))
    return pl.pallas_call(
        functools.partial(_hgrn_kernel, seq=seq),
        grid=(b, n_hp),
        in_specs=[pl.BlockSpec((2, 2, LANES), lambda bi, h: (0, 0, h)),
                  col(0), col(1), col(2), col(3), col(4),
                  pl.BlockSpec((1, LANES), lambda bi, h: (0, 0))],
        out_specs=pl.BlockSpec((None, seq, LANES), lambda bi, h: (bi, 0, h)),
        out_shape=jax.ShapeDtypeStruct((b, seq, HG_WIDTH), bf16),
        scratch_shapes=[pltpu.VMEM((seq, LANES), f32)] * 2,
        compiler_params=_cparams(("arbitrary", "arbitrary")),
        name="hgrn2_bidir",
    )(lower_bounds, proj3, proj3, proj3, proj3, proj3, hn2)


OUT_TM = 512


def _outproj_kernel(x_ref, att_ref, hg_ref, w_ref, g_ref, h_ref, xn_ref):
    h = (x_ref[...]
         + jnp.dot(att_ref[...], w_ref[0:ATT_WIDTH, :], preferred_element_type=f32)
         + jnp.dot(hg_ref[...], w_ref[ATT_WIDTH:, :], preferred_element_type=f32))
    h_ref[...] = h
    ms = jnp.mean(h * h, axis=-1, keepdims=True)
    xn_ref[...] = (h * lax.rsqrt(ms + NORM_EPS)) * g_ref[...]


def out_proj(x2d, att2d, hg2d, w_bf16, gain):
    n = x2d.shape[0]
    row = lambda w: pl.BlockSpec((OUT_TM, w), lambda i: (i, 0))
    return pl.pallas_call(
        _outproj_kernel,
        grid=(n // OUT_TM,),
        in_specs=[row(D_MODEL), row(ATT_WIDTH), row(HG_WIDTH),
                  pl.BlockSpec((D_MODEL, D_MODEL), lambda i: (0, 0)),
                  pl.BlockSpec((1, D_MODEL), lambda i: (0, 0))],
        out_specs=[row(D_MODEL), row(D_MODEL)],
        out_shape=[jax.ShapeDtypeStruct((n, D_MODEL), f32)] * 2,
        compiler_params=_cparams(("arbitrary",)),
        name="out_proj",
    )(x2d, att2d, hg2d, w_bf16, gain)


RT_T = 1024
PEER_TT = 128
PEER_J = PEER_HEADS * PEER_TOPK
BIG = 1.0e9


def _top16(s, posf):
    vals, poss = [], []
    for _ in range(PEER_TOPK):
        m = jnp.max(s, axis=0, keepdims=True)
        pos = jnp.min(jnp.where(s == m, posf, BIG), axis=0, keepdims=True)
        vals.append(m)
        poss.append(pos)
        s = jnp.where(posf == pos, NEG_INF, s)
    return vals, poss


def _stack16(rows):
    r16 = lax.broadcasted_iota(i32, (PEER_TOPK, rows[0].shape[1]), 0)
    out = jnp.zeros((PEER_TOPK, rows[0].shape[1]), f32)
    for kk, rw in enumerate(rows):
        out = jnp.where(r16 == kk, rw, out)
    return out


def _route_head(s0, s1):
    cdim = s0.shape[1]
    keypos = lax.broadcasted_iota(i32, (PEER_N_KEYS, cdim), 0).astype(f32)
    v0, p0 = _top16(s0, keypos)
    v1, p1 = _top16(s1, keypos)
    sv0, sp0, sv1, sp1 = _stack16(v0), _stack16(p0), _stack16(v1), _stack16(p1)

    r8 = lax.broadcasted_iota(i32, (SUBLANES, cdim), 0)
    groups_v = [v0[0] + sv1]
    groups_e = [p0[0] * PEER_N_KEYS + sp1]
    groups_p = [lax.broadcasted_iota(i32, (PEER_TOPK, cdim), 0).astype(f32)]
    for i in range(1, 8):
        nj = PEER_TOPK // (i + 1)
        ok = r8 < nj
        groups_v.append(jnp.where(ok, v0[i] + sv1[0:SUBLANES], NEG_INF))
        groups_e.append(p0[i] * PEER_N_KEYS + sp1[0:SUBLANES])
        groups_p.append((r8 + i * PEER_TOPK).astype(f32))
    groups_v.append(sv0[SUBLANES:] + v1[0])
    groups_e.append(sp0[SUBLANES:] * PEER_N_KEYS + p1[0])
    groups_p.append(((r8 + SUBLANES) * PEER_TOPK).astype(f32))
    cand = jnp.concatenate(groups_v, axis=0)
    cexp = jnp.concatenate(groups_e, axis=0)
    cpos = jnp.concatenate(groups_p, axis=0)

    vals, exps = [], []
    for _ in range(PEER_TOPK):
        m = jnp.max(cand, axis=0, keepdims=True)
        pos = jnp.min(jnp.where(cand == m, cpos, BIG), axis=0, keepdims=True)
        sel = cpos == pos
        exps.append(jnp.max(jnp.where(sel, cexp, -1.0), axis=0, keepdims=True))
        vals.append(m)
        cand = jnp.where(sel, NEG_INF, cand)
    sv = _stack16(vals)
    ex = jnp.exp(sv - vals[0])
    gate = ex / jnp.sum(ex, axis=0, keepdims=True)
    return _stack16(exps), gate


def _route_kernel(xn_ref, wqt_ref, keys_ref, idx_ref, gate_ref, qt_scr, eid_scr):
    xb = xn_ref[...].astype(bf16)
    qt_scr[...] = lax.dot_general(wqt_ref[...], xb, (((1,), (1,)), ((), ())),
                                  preferred_element_type=f32)

    def head_body(h, carry):
        for ci in range(RT_T // PEER_TT):
            cols = slice(ci * PEER_TT, (ci + 1) * PEER_TT)
            halves = []
            for p in range(2):
                r0 = pl.multiple_of((2 * h + p) * PEER_HALF, PEER_HALF)
                qh = qt_scr[pl.ds(r0, PEER_HALF), cols].astype(bf16)
                halves.append(jnp.dot(keys_ref[2 * h + p], qh, preferred_element_type=f32))
            eid, gate = _route_head(halves[0], halves[1])
            rows = pl.ds(pl.multiple_of(h * PEER_TOPK, PEER_TOPK), PEER_TOPK)
            eid_scr[ci, rows, :] = eid
            gate_ref[ci, rows, :] = gate
        return carry
    lax.fori_loop(0, PEER_HEADS, head_body, 0)
    for ci in range(RT_T // PEER_TT):
        idx_ref[ci] = (eid_scr[ci].T * float(SUBLANES // 2)).astype(i32)


def route(xn2d, wqt_bf16, keys_bf16):
    n = xn2d.shape[0]
    n_tiles = n // PEER_TT
    per = RT_T // PEER_TT
    ospec = pl.BlockSpec((per, PEER_J, PEER_TT), lambda i: (i, 0, 0))
    return pl.pallas_call(
        _route_kernel,
        grid=(n // RT_T,),
        in_specs=[pl.BlockSpec((RT_T, D_MODEL), lambda i: (i, 0)),
                  pl.BlockSpec((2 * PEER_HEADS * PEER_HALF, D_MODEL), lambda i: (0, 0)),
                  pl.BlockSpec((2 * PEER_HEADS, PEER_N_KEYS, PEER_HALF), lambda i: (0, 0, 0))],
        out_specs=[pl.BlockSpec((per, PEER_TT, PEER_J), lambda i: (i, 0, 0)), ospec],
        out_shape=[jax.ShapeDtypeStruct((n_tiles, PEER_TT, PEER_J), i32),
                   jax.ShapeDtypeStruct((n_tiles, PEER_J, PEER_TT), f32)],
        scratch_shapes=[pltpu.VMEM((2 * PEER_HEADS * PEER_HALF, RT_T), f32),
                        pltpu.VMEM((per, PEER_J, PEER_TT), f32)],
        compiler_params=_cparams(("arbitrary",)),
        name="peer_route",
    )(xn2d, wqt_bf16, keys_bf16)


PEER_FLAT = PEER_TT * PEER_J
PACK_ROWS = SUBLANES // 2
PEER_NACC = 4


def _resident_tile_offset(step, n_steps, idx_hbm, idx_smem, sem):
    def copy(tile, slot):
        dst = idx_smem.at[pl.ds(pl.multiple_of(slot * PEER_FLAT, PEER_FLAT), PEER_FLAT)]
        return pltpu.make_async_copy(idx_hbm.at[tile], dst, sem.at[slot])

    slot = step % 2

    @pl.when(step == 0)
    def _():
        copy(0, 0).start()

    copy(step, slot).wait()

    @pl.when(step + 1 < n_steps)
    def _():
        copy(step + 1, 1 - slot).start()
    return slot * PEER_FLAT


def _sublane_fold(x, y, s, sub):
    clear = (sub & s) == 0
    a = jnp.where(clear, x, y)
    b = jnp.where(clear, y, x)
    if 2 * s == SUBLANES:
        partner = pltpu.roll(b, s, axis=0)
    else:
        partner = jnp.where(clear, pltpu.roll(b, SUBLANES - s, axis=0), pltpu.roll(b, s, axis=0))
    return a + partner


def _expert_row(tab_ref, row4):
    words = tab_ref[pl.ds(pl.multiple_of(row4, PACK_ROWS), PACK_ROWS), :]
    return pltpu.bitcast(words, bf16).astype(f32)


_BITREV3 = (0, 4, 2, 6, 1, 5, 3, 7)
PEER_GROUPS = PEER_J // SUBLANES


def _peer_u_kernel(idx_hbm, tab_ref, x_ref, gate_ref, w_ref, idx_smem, sem, h_scr):
    off = _resident_tile_offset(pl.program_id(0), pl.num_programs(0), idx_hbm, idx_smem, sem)
    lane = lax.broadcasted_iota(i32, (SUBLANES, PEER_TT), 1)
    sub = lax.broadcasted_iota(i32, (SUBLANES, LANES), 0)
    h_scr[...] = jnp.zeros((PEER_J, PEER_TT), f32)

    def retire(t_done, partial):
        for g in range(PEER_GROUPS):
            col = jnp.sum(partial[g], axis=-1, keepdims=True)
            rows = slice(g * SUBLANES, (g + 1) * SUBLANES)
            h_scr[rows, :] = jnp.where(lane == t_done, col, h_scr[rows, :])

    def tok_body(t, prev):
        retire(t - 1, prev)
        xt = x_ref[t]
        base = off + t * PEER_J
        partial = []
        for g in range(PEER_GROUPS):
            p = [_expert_row(tab_ref, idx_smem[base + g * SUBLANES + jj]) * xt for jj in _BITREV3]
            q = [_sublane_fold(p[2 * i], p[2 * i + 1], 4, sub) for i in range(4)]
            r = [_sublane_fold(q[2 * i], q[2 * i + 1], 2, sub) for i in range(2)]
            partial.append(_sublane_fold(r[0], r[1], 1, sub))
        return tuple(partial)
    zero = jnp.zeros((SUBLANES, LANES), f32)
    last = lax.fori_loop(0, PEER_TT, tok_body, (zero,) * PEER_GROUPS)
    retire(PEER_TT - 1, last)
    w_ref[...] = gate_ref[...] * jax.nn.gelu(h_scr[...])


def peer_u_pass(idx2, gate3, tab3, xn3):
    n_tiles = idx2.shape[0]
    tspec = pl.BlockSpec((None, PEER_J, PEER_TT), lambda i: (i, 0, 0))
    return pl.pallas_call(
        _peer_u_kernel,
        grid=(n_tiles,),
        in_specs=[pl.BlockSpec(memory_space=pl.ANY),
                  pl.BlockSpec(memory_space=pltpu.VMEM),
                  pl.BlockSpec((PEER_TT, SUBLANES, LANES), lambda i: (i, 0, 0)),
                  tspec],
        out_specs=tspec,
        out_shape=jax.ShapeDtypeStruct((n_tiles, PEER_J, PEER_TT), f32),
        scratch_shapes=[pltpu.SMEM((2 * PEER_FLAT,), i32),
                        pltpu.SemaphoreType.DMA((2,)),
                        pltpu.VMEM((PEER_J, PEER_TT), f32)],
        compiler_params=_cparams(("arbitrary",)),
        name="peer_u",
    )(idx2, tab3, xn3, gate3)


def _peer_v_kernel(idx_hbm, tab_ref, w_ref, h_ref, y_ref, idx_smem, sem):
    off = _resident_tile_offset(pl.program_id(0), pl.num_programs(0), idx_hbm, idx_smem, sem)
    lane = lax.broadcasted_iota(i32, (SUBLANES, PEER_TT), 1)

    def lane_splat(t):
        out = []
        for g in range(PEER_GROUPS):
            wg = w_ref[g * SUBLANES:(g + 1) * SUBLANES, :]
            col = jnp.sum(jnp.where(lane == t, wg, 0.0), axis=-1, keepdims=True)
            out.append(jnp.broadcast_to(col, (SUBLANES, LANES)))
        return tuple(out)

    def tok_body(t, wsplat):
        nxt = lane_splat(t + 1)
        base = off + t * PEER_J
        zero = jnp.zeros((SUBLANES, LANES), f32)
        accs = [h_ref[t]] + [zero] * (PEER_NACC - 1)
        for g in range(PEER_GROUPS):
            p = []
            for i in range(SUBLANES):
                row = _expert_row(tab_ref, idx_smem[base + g * SUBLANES + i])
                p.append(jnp.broadcast_to(wsplat[g][i:i + 1, :], (SUBLANES, LANES)) * row)
            s = ((p[0] + p[1]) + (p[2] + p[3])) + ((p[4] + p[5]) + (p[6] + p[7]))
            accs[g % PEER_NACC] = accs[g % PEER_NACC] + s
        y_ref[t] = (accs[0] + accs[1]) + (accs[2] + accs[3])
        return nxt
    lax.fori_loop(0, PEER_TT, tok_body, lane_splat(0))


def peer_v_pass(idx2, w3, tab3, h3):
    n_tiles = idx2.shape[0]
    xspec = pl.BlockSpec((PEER_TT, SUBLANES, LANES), lambda i: (i, 0, 0))
    return pl.pallas_call(
        _peer_v_kernel,
        grid=(n_tiles,),
        in_specs=[pl.BlockSpec(memory_space=pl.ANY),
                  pl.BlockSpec(memory_space=pltpu.VMEM),
                  pl.BlockSpec((None, PEER_J, PEER_TT), lambda i: (i, 0, 0)),
                  xspec],
        out_specs=xspec,
        out_shape=jax.ShapeDtypeStruct(h3.shape, f32),
        scratch_shapes=[pltpu.SMEM((2 * PEER_FLAT,), i32),
                        pltpu.SemaphoreType.DMA((2,))],
        compiler_params=_cparams(("arbitrary",)),
        name="peer_v",
    )(idx2, tab3, w3, h3)


def _trunk(x, p):
    b, seq, d = x.shape
    n = b * seq
    x2d = x.reshape(n, d)
    proj3 = in_proj(x2d, p["norm1"], p["w_in"]).reshape(b, seq, N_IN_COLS)
    att = attention(proj3, p["slopes"], p["q_gain"], p["k_gain"])
    hg = hgrn(proj3, p["lower_bounds"], p["hg_gain"])
    h1, xn2 = out_proj(x2d, att.reshape(n, ATT_WIDTH), hg.reshape(n, HG_WIDTH), p["w_out"], p["norm2"])
    idx3, gate3 = route(xn2, p["wq_t"], p["keys"])
    idx2 = idx3.reshape(-1, PEER_FLAT)
    w3 = peer_u_pass(idx2, gate3, p["u_tab"], xn2.reshape(n, SUBLANES, LANES))
    y3 = peer_v_pass(idx2, w3, p["v_tab"], h1.reshape(n, SUBLANES, LANES))
    return y3.reshape(b, seq, d)


def _pack_table(tab):
    n_exp = tab.shape[0]
    pairs = tab.astype(bf16).reshape(n_exp, SUBLANES // 2, 2, LANES).transpose(0, 1, 3, 2)
    return lax.bitcast_convert_type(pairs, jnp.uint32).reshape(n_exp * (SUBLANES // 2), LANES)


def kernel(x_prompt, x_sample, norm1, w_in, q_norm, k_norm, lower_bounds, hg_norm, w_out, norm2,
           peer_wq, peer_keys, peer_u, peer_v):
    n_heads = ATT_WIDTH // HEAD_DIM
    p = {
        "norm1": norm1[0].reshape(1, D_MODEL),
        "w_in": w_in[0].astype(bf16),
        "q_gain": jnp.tile(q_norm[0], 2).reshape(1, LANES),
        "k_gain": jnp.tile(k_norm[0], 2).reshape(1, LANES),
        "hg_gain": jnp.tile(hg_norm[0], 2).reshape(1, LANES),
        "slopes": 2.0 ** (-8.0 * jnp.arange(1, n_heads + 1, dtype=f32) / n_heads),
        "lower_bounds": lower_bounds.astype(f32),
        "w_out": w_out[0].astype(bf16),
        "norm2": norm2[0].reshape(1, D_MODEL),
        "wq_t": peer_wq[0].T.astype(bf16),
        "keys": peer_keys[0].reshape(2 * PEER_HEADS, PEER_N_KEYS, PEER_HALF).astype(bf16),
        "u_tab": _pack_table(peer_u[0]),
        "v_tab": _pack_table(peer_v[0]),
    }
    return (_trunk(x_prompt, p), _trunk(x_sample, p))
```

```python
import functools

import jax
import jax.numpy as jnp
from jax import lax
from jax.experimental import pallas as pl
from jax.experimental.pallas import tpu as pltpu

f32 = jnp.float32
bf16 = jnp.bfloat16
i32 = jnp.int32

D_MODEL = 1024
ATT_WIDTH = 512
HG_WIDTH = 512
HEAD_DIM = 64
N_IN_COLS = 3 * ATT_WIDTH + 5 * HG_WIDTH
DILATIONS = (16, 4, 1)
KEYS_PER_SIDE = 64
HG_CHUNK = 64
PEER_HEADS = 8
PEER_N_KEYS = 128
PEER_TOPK = 16
PEER_HALF = 128
NORM_EPS = 1e-6

LANES = 128
SUBLANES = 8
VMEM_LIMIT = 56 * 1024 * 1024
NEG_INF = float("-inf")


def _cparams(sem=None, flags=None):
    return pltpu.CompilerParams(dimension_semantics=sem, vmem_limit_bytes=VMEM_LIMIT, flags=flags)


IN_TM = 512


def _inproj_kernel(x_ref, g_ref, w_ref, o_ref):
    x = x_ref[...]
    ms = jnp.mean(x * x, axis=-1, keepdims=True)
    xn = ((x * lax.rsqrt(ms + NORM_EPS)) * g_ref[...]).astype(bf16)
    tn = 1024
    for c in range(N_IN_COLS // tn):
        o_ref[:, c * tn:(c + 1) * tn] = jnp.dot(xn, w_ref[:, c * tn:(c + 1) * tn],
                                                preferred_element_type=f32)


def in_proj(x2d, gain, w_bf16):
    n = x2d.shape[0]
    return pl.pallas_call(
        _inproj_kernel,
        grid=(n // IN_TM,),
        in_specs=[pl.BlockSpec((IN_TM, D_MODEL), lambda i: (i, 0)),
                  pl.BlockSpec((1, D_MODEL), lambda i: (0, 0)),
                  pl.BlockSpec((D_MODEL, N_IN_COLS), lambda i: (0, 0))],
        out_specs=pl.BlockSpec((IN_TM, N_IN_COLS), lambda i: (i, 0)),
        out_shape=jax.ShapeDtypeStruct((n, N_IN_COLS), f32),
        compiler_params=_cparams(("arbitrary",)),
        name="in_proj",
    )(x2d, gain, w_bf16)


def _pair_rmsnorm(x, gain, same_head):
    x2 = x * x
    hi = x2.astype(bf16)
    lo = (x2 - hi.astype(f32)).astype(bf16)
    ss = (jnp.dot(hi, same_head, preferred_element_type=f32)
          + jnp.dot(lo, same_head, preferred_element_type=f32))
    return (x * lax.rsqrt(ss * (1.0 / HEAD_DIM) + NORM_EPS)) * gain


def _same_head_matrix():
    r = lax.broadcasted_iota(i32, (LANES, LANES), 0) < HEAD_DIM
    c = lax.broadcasted_iota(i32, (LANES, LANES), 1) < HEAD_DIM
    return (r == c).astype(bf16)


ATT_QB = 128
ATT_KW = ATT_QB + 2 * KEYS_PER_SIDE
ATT_CP = 256
ATT_CPU = 4
ATT_NB = 8


def _attn_kernel(slopes_ref, q_ref, k_ref, v_ref, qg_ref, kg_ref, o_ref,
                 qc, kc, vc, oacc, macc, lacc, *, seq):
    hp = pl.program_id(1)
    lane = lax.broadcasted_iota(i32, (1, LANES), 1)
    head0 = lane < HEAD_DIM
    same_head = _same_head_matrix()
    qg = qg_ref[...]
    kg = kg_ref[...]
    zpad = jnp.zeros((KEYS_PER_SIDE, LANES), bf16)
    for ref in (qc, kc, vc):
        ref[0:KEYS_PER_SIDE, :] = zpad
        ref[seq + KEYS_PER_SIDE:seq + 2 * KEYS_PER_SIDE, :] = zpad
    slope = (slopes_ref[2 * hp], slopes_ref[2 * hp + 1])

    iq = lax.broadcasted_iota(i32, (ATT_QB, ATT_KW), 0)
    jj = lax.broadcasted_iota(i32, (ATT_QB, ATT_KW), 1)
    rel = jnp.abs(iq + KEYS_PER_SIDE - jj)
    band = rel <= KEYS_PER_SIDE

    for pat, dil in enumerate(DILATIONS):
        cls_len = seq // dil
        shift = cls_len.bit_length() - 1
        cp = min(ATT_CP, cls_len)
        n_cp = cls_len // cp

        def copy_body(i, carry, dil=dil, cp=cp, n_cp=n_cp):
            for u in range(ATT_CPU):
                chunk = ATT_CPU * i + u
                if dil == 1:
                    src = pl.ds(pl.multiple_of(chunk * cp, cp), cp)
                else:
                    r = lax.shift_right_logical(chunk, n_cp.bit_length() - 1)
                    c = chunk & (n_cp - 1)
                    src = pl.ds(r + dil * cp * c, cp, stride=dil)
                dst = pl.ds(pl.multiple_of(KEYS_PER_SIDE + chunk * cp, KEYS_PER_SIDE), cp)
                qn = _pair_rmsnorm(q_ref[src, :], qg, same_head) * (HEAD_DIM ** -0.5)
                qc[dst, :] = qn.astype(bf16)
                kc[dst, :] = _pair_rmsnorm(k_ref[src, :], kg, same_head).astype(bf16)
                vc[dst, :] = v_ref[src, :].astype(bf16)
            return carry
        lax.fori_loop(0, seq // (cp * ATT_CPU), copy_body, 0)

        relf = (dil * rel).astype(f32)
        bias = [jnp.where(band, -slope[a] * relf, NEG_INF) for a in range(2)]

        def one_block(i, dil=dil, cls_len=cls_len, shift=shift, bias=bias):
            r0 = pl.multiple_of(i * ATT_QB, ATT_QB)
            cls = lax.shift_right_logical(r0, shift)
            lo = cls * cls_len - (r0 - KEYS_PER_SIDE)
            valid = (jj >= lo) & (jj < lo + cls_len)
            qb = qc[pl.ds(pl.multiple_of(r0 + KEYS_PER_SIDE, KEYS_PER_SIDE), ATT_QB), :]
            kw = kc[pl.ds(r0, ATT_KW), :]
            vw = vc[pl.ds(r0, ATT_KW), :]
            res = []
            for a in range(2):
                hm = head0 if a == 0 else jnp.logical_not(head0)
                qa = jnp.where(hm, qb, jnp.zeros_like(qb))
                s = lax.dot_general(qa, kw, (((1,), (1,)), ((), ())), preferred_element_type=f32)
                s = jnp.where(valid, s + bias[a], NEG_INF)
                m = jnp.max(s, axis=-1, keepdims=True)
                p = jnp.exp(s - m)
                l = jnp.sum(p, axis=-1, keepdims=True)
                o = jnp.dot(p.astype(bf16), vw, preferred_element_type=f32)
                res.append((o, m, l))
            o = jnp.where(head0, res[0][0], res[1][0])
            m = jnp.where(head0, res[0][1], res[1][1])
            l = jnp.where(head0, res[0][2], res[1][2])
            if dil == 1:
                tok = pl.ds(r0, ATT_QB)
            else:
                tok = pl.ds((r0 - cls * cls_len) * dil + cls, ATT_QB, stride=dil)
            return tok, o, m, l

        def blocks_body(i, carry, pat=pat, one_block=one_block):
            done = [one_block(ATT_NB * i + u) for u in range(ATT_NB)]
            if pat == 0:
                for tok, o, m, l in done:
                    oacc[tok, :] = o
                    macc[tok, :] = m
                    lacc[tok, :] = l
            else:
                old = [(macc[tok, :], oacc[tok, :], lacc[tok, :]) for tok, _, _, _ in done]
                for (tok, o, m, l), (m_old, o_old, l_old) in zip(done, old):
                    m_new = jnp.maximum(m_old, m)
                    a_old = jnp.exp(m_old - m_new)
                    a_new = jnp.exp(m - m_new)
                    oacc[tok, :] = a_old * o_old + a_new * o
                    lacc[tok, :] = a_old * l_old + a_new * l
                    macc[tok, :] = m_new
            return carry
        lax.fori_loop(0, seq // (ATT_QB * ATT_NB), blocks_body, 0)

    def fin_body(c, carry):
        rows = pl.ds(pl.multiple_of(c * ATT_CP, ATT_CP), ATT_CP)
        o_ref[rows, :] = (oacc[rows, :] / lacc[rows, :]).astype(bf16)
        return carry
    lax.fori_loop(0, seq // ATT_CP, fin_body, 0)


def attention(proj3, slopes, qg2, kg2):
    b, seq, _ = proj3.shape
    n_hp = ATT_WIDTH // LANES

    def col(off):
        return pl.BlockSpec((None, seq, LANES), lambda bi, h, off=off: (bi, 0, off + h),
                            pipeline_mode=pl.Buffered(1))
    gspec = pl.BlockSpec((1, LANES), lambda bi, h: (0, 0))
    return pl.pallas_call(
        functools.partial(_attn_kernel, seq=seq),
        grid=(b, n_hp),
        in_specs=[pl.BlockSpec(memory_space=pltpu.SMEM),
                  col(0), col(n_hp), col(2 * n_hp), gspec, gspec],
        out_specs=pl.BlockSpec((None, seq, LANES), lambda bi, h: (bi, 0, h)),
        out_shape=jax.ShapeDtypeStruct((b, seq, ATT_WIDTH), bf16),
        scratch_shapes=[pltpu.VMEM((seq + 2 * KEYS_PER_SIDE, LANES), bf16)] * 3
                       + [pltpu.VMEM((seq, LANES), f32)] * 3,
        compiler_params=_cparams(("arbitrary", "arbitrary")),
        name="dilated_attention",
    )(slopes, proj3, proj3, proj3, qg2, kg2)


HG_LEVELS = 6
HG_PER_ITER = 8


def _shift_rows(x, j, up):
    n = x.shape[0]
    return pltpu.roll(x, (n - j) % n if up else j, axis=0)


def _hg_chunk(q, k, v, lf, rev, head0_rows, blockdiag):
    c = HG_CHUNK
    row = lax.broadcasted_iota(i32, (c, LANES), 0)
    cum = lf
    for lv in range(HG_LEVELS):
        j = 1 << lv
        sh = _shift_rows(cum, j, up=rev)
        ok = (row < c - j) if rev else (row >= j)
        cum = cum + jnp.where(ok, sh, 0.0)

    trow = lax.broadcasted_iota(i32, (2 * c, c), 0) & (c - 1)
    scol = lax.broadcasted_iota(i32, (2 * c, c), 1)
    differ = trow ^ scol
    q_h0 = jnp.where(head0_rows, q, 0.0)
    q_h1 = q - q_h0

    def pair_scores(a_decay, b_mat):
        a2 = jnp.concatenate([q_h0 * a_decay, q_h1 * a_decay], axis=0) if a_decay is not None else \
            jnp.concatenate([q_h0, q_h1], axis=0)
        return lax.dot_general(a2.astype(bf16), b_mat.astype(bf16), (((1,), (1,)), ((), ())),
                               preferred_element_type=f32)

    scores = pair_scores(None, k)
    edge = cum
    for lv in range(1, HG_LEVELS + 1):
        h = 1 << (lv - 1)
        if h > 1:
            hh = h >> 1
            sh = _shift_rows(edge, hh, up=not rev)
            take = ((row & hh) == 0) != rev
            edge = jnp.where(take, sh, edge)
        far = ((row & h) != 0) != rev
        edge_prev = _shift_rows(edge, h, up=rev)
        a_decay = jnp.exp(jnp.where(far, cum - edge_prev, NEG_INF))
        b_mat = k * jnp.exp(jnp.where(far, NEG_INF, edge - cum))
        scores = jnp.where(differ < h, scores, pair_scores(a_decay, b_mat))

    hh = c >> 1
    sh = _shift_rows(edge, hh, up=not rev)
    total = jnp.where(((row & hh) == 0) != rev, sh, edge)

    vb = v.astype(bf16)
    scb = scores.astype(bf16)
    intra0 = jnp.dot(scb[0:c], vb, preferred_element_type=f32)
    intra1 = jnp.dot(scb[c:2 * c], vb, preferred_element_type=f32)
    intra = jnp.where(head0_rows, intra0, intra1)
    qe = (q * jnp.exp(cum)).astype(bf16)
    kd = (k * jnp.exp(total - cum)).astype(bf16)
    upd = lax.dot_general(vb, kd, (((0,), (0,)), ((), ())), preferred_element_type=f32)
    return intra, qe, jnp.where(blockdiag, upd, 0.0), jnp.exp(total[0:1, :])


def _hg_state_step(st, intra, qe, upd, decay):
    inter = lax.dot_general(qe, st.astype(bf16), (((1,), (1,)), ((), ())), preferred_element_type=f32)
    return inter + intra, st * decay + upd


def _hgrn_kernel(lb_ref, q_ref, zf_ref, zb_ref, i_ref, g_ref, hn_ref, o_ref, of_scr, ob_scr, *, seq):
    c = HG_CHUNK
    head0_rows = lax.broadcasted_iota(i32, (c, LANES), 1) < HEAD_DIM
    r2 = lax.broadcasted_iota(i32, (LANES, LANES), 0) < HEAD_DIM
    c2 = lax.broadcasted_iota(i32, (LANES, LANES), 1) < HEAD_DIM
    blockdiag = r2 == c2

    lbs = []
    for d in range(2):
        a0 = lb_ref[d, 0:1, :]
        a1 = lb_ref[d, 1:2, :]
        mx = jnp.maximum(a0, a1)
        e0 = jnp.exp(a0 - mx)
        e1 = jnp.exp(a1 - mx)
        lbs.append(e0 / (e0 + e1))

    n_chunks = seq // c

    def gates(z, lb):
        f = lb + (1.0 - lb) * jax.nn.sigmoid(z)
        return jnp.log(f), 1.0 - f

    def scan_body(it, states):
        st_f, st_b = states
        work = []
        for u in range(HG_PER_ITER):
            ci = it * HG_PER_ITER + u
            rows_f = pl.ds(pl.multiple_of(ci * c, c), c)
            rows_b = pl.ds(pl.multiple_of((n_chunks - 1 - ci) * c, c), c)
            lf_f, k_f = gates(zf_ref[rows_f, :], lbs[0])
            lf_b, k_b = gates(zb_ref[rows_b, :], lbs[1])
            work.append((rows_f, _hg_chunk(q_ref[rows_f, :], k_f, i_ref[rows_f, :], lf_f, False,
                                           head0_rows, blockdiag),
                         rows_b, _hg_chunk(q_ref[rows_b, :], k_b, i_ref[rows_b, :], lf_b, True,
                                           head0_rows, blockdiag)))
        for rows_f, part_f, rows_b, part_b in work:
            o_f, st_f = _hg_state_step(st_f, *part_f)
            o_b, st_b = _hg_state_step(st_b, *part_b)
            of_scr[rows_f, :] = o_f
            ob_scr[rows_b, :] = o_b
        return st_f, st_b
    zero_state = jnp.zeros((LANES, LANES), f32)
    lax.fori_loop(0, n_chunks // HG_PER_ITER, scan_body, (zero_state, zero_state))

    hn = hn_ref[...]
    same_head = blockdiag.astype(bf16)
    fin = 4 * c

    def fin_body(ci, carry):
        rows = pl.ds(pl.multiple_of(ci * fin, fin), fin)
        y = _pair_rmsnorm(of_scr[rows, :] + ob_scr[rows, :], hn, same_head)
        o_ref[rows, :] = (y * jax.nn.silu(g_ref[rows, :])).astype(bf16)
        return carry
    lax.fori_loop(0, seq // fin, fin_body, 0)


def hgrn(proj3, lower_bounds, hn2):
    b, seq, _ = proj3.shape
    n_hp = HG_WIDTH // LANES
    base = 3 * ATT_WIDTH // LANES

    def col(k):
        return pl.BlockSpec((None, seq, LANES), lambda bi, h, k=k: (bi, 0, base + k * n_hp + h),
                            pipeline_mode=pl.Buffered(1))
    return pl.pallas_call(
        functools.partial(_hgrn_kernel, seq=seq),
        grid=(b, n_hp),
        in_specs=[pl.BlockSpec((2, 2, LANES), lambda bi, h: (0, 0, h)),
                  col(0), col(1), col(2), col(3), col(4),
                  pl.BlockSpec((1, LANES), lambda bi, h: (0, 0))],
        out_specs=pl.BlockSpec((None, seq, LANES), lambda bi, h: (bi, 0, h)),
        out_shape=jax.ShapeDtypeStruct((b, seq, HG_WIDTH), bf16),
        scratch_shapes=[pltpu.VMEM((seq, LANES), f32)] * 2,
        compiler_params=_cparams(("arbitrary", "arbitrary")),
        name="hgrn2_bidir",
    )(lower_bounds, proj3, proj3, proj3, proj3, proj3, hn2)


OUT_TM = 512


def _outproj_kernel(x_ref, att_ref, hg_ref, w_ref, g_ref, h_ref, xn_ref):
    h = (x_ref[...]
         + jnp.dot(att_ref[...], w_ref[0:ATT_WIDTH, :], preferred_element_type=f32)
         + jnp.dot(hg_ref[...], w_ref[ATT_WIDTH:, :], preferred_element_type=f32))
    h_ref[...] = h
    ms = jnp.mean(h * h, axis=-1, keepdims=True)
    xn_ref[...] = (h * lax.rsqrt(ms + NORM_EPS)) * g_ref[...]


def out_proj(x2d, att2d, hg2d, w_bf16, gain):
    n = x2d.shape[0]
    row = lambda w: pl.BlockSpec((OUT_TM, w), lambda i: (i, 0))
    return pl.pallas_call(
        _outproj_kernel,
        grid=(n // OUT_TM,),
        in_specs=[row(D_MODEL), row(ATT_WIDTH), row(HG_WIDTH),
                  pl.BlockSpec((D_MODEL, D_MODEL), lambda i: (0, 0)),
                  pl.BlockSpec((1, D_MODEL), lambda i: (0, 0))],
        out_specs=[row(D_MODEL), row(D_MODEL)],
        out_shape=[jax.ShapeDtypeStruct((n, D_MODEL), f32)] * 2,
        compiler_params=_cparams(("arbitrary",)),
        name="out_proj",
    )(x2d, att2d, hg2d, w_bf16, gain)


RT_T = 1024
PEER_TT = 128
PEER_J = PEER_HEADS * PEER_TOPK
BIG = 1.0e9


def _top16(s, posf):
    vals, poss = [], []
    for _ in range(PEER_TOPK):
        m = jnp.max(s, axis=0, keepdims=True)
        pos = jnp.min(jnp.where(s == m, posf, BIG), axis=0, keepdims=True)
        vals.append(m)
        poss.append(pos)
        s = jnp.where(posf == pos, NEG_INF, s)
    return vals, poss


def _stack16(rows):
    r16 = lax.broadcasted_iota(i32, (PEER_TOPK, rows[0].shape[1]), 0)
    out = jnp.zeros((PEER_TOPK, rows[0].shape[1]), f32)
    for kk, rw in enumerate(rows):
        out = jnp.where(r16 == kk, rw, out)
    return out


def _route_head(s0, s1):
    cdim = s0.shape[1]
    keypos = lax.broadcasted_iota(i32, (PEER_N_KEYS, cdim), 0).astype(f32)
    v0, p0 = _top16(s0, keypos)
    v1, p1 = _top16(s1, keypos)
    sv0, sp0, sv1, sp1 = _stack16(v0), _stack16(p0), _stack16(v1), _stack16(p1)

    r8 = lax.broadcasted_iota(i32, (SUBLANES, cdim), 0)
    groups_v = [v0[0] + sv1]
    groups_e = [p0[0] * PEER_N_KEYS + sp1]
    groups_p = [lax.broadcasted_iota(i32, (PEER_TOPK, cdim), 0).astype(f32)]
    for i in range(1, 8):
        nj = PEER_TOPK // (i + 1)
        ok = r8 < nj
        groups_v.append(jnp.where(ok, v0[i] + sv1[0:SUBLANES], NEG_INF))
        groups_e.append(p0[i] * PEER_N_KEYS + sp1[0:SUBLANES])
        groups_p.append((r8 + i * PEER_TOPK).astype(f32))
    groups_v.append(sv0[SUBLANES:] + v1[0])
    groups_e.append(sp0[SUBLANES:] * PEER_N_KEYS + p1[0])
    groups_p.append(((r8 + SUBLANES) * PEER_TOPK).astype(f32))
    cand = jnp.concatenate(groups_v, axis=0)
    cexp = jnp.concatenate(groups_e, axis=0)
    cpos = jnp.concatenate(groups_p, axis=0)

    vals, exps = [], []
    for _ in range(PEER_TOPK):
        m = jnp.max(cand, axis=0, keepdims=True)
        pos = jnp.min(jnp.where(cand == m, cpos, BIG), axis=0, keepdims=True)
        sel = cpos == pos
        exps.append(jnp.max(jnp.where(sel, cexp, -1.0), axis=0, keepdims=True))
        vals.append(m)
        cand = jnp.where(sel, NEG_INF, cand)
    sv = _stack16(vals)
    ex = jnp.exp(sv - vals[0])
    gate = ex / jnp.sum(ex, axis=0, keepdims=True)
    return _stack16(exps), gate


def _route_kernel(xn_ref, wqt_ref, keys_ref, idx_ref, gate_ref, qt_scr, eid_scr):
    xb = xn_ref[...].astype(bf16)
    qt_scr[...] = lax.dot_general(wqt_ref[...], xb, (((1,), (1,)), ((), ())),
                                  preferred_element_type=f32)

    def head_body(h, carry):
        for ci in range(RT_T // PEER_TT):
            cols = slice(ci * PEER_TT, (ci + 1) * PEER_TT)
            halves = []
            for p in range(2):
                r0 = pl.multiple_of((2 * h + p) * PEER_HALF, PEER_HALF)
                qh = qt_scr[pl.ds(r0, PEER_HALF), cols].astype(bf16)
                halves.append(jnp.dot(keys_ref[2 * h + p], qh, preferred_element_type=f32))
            eid, gate = _route_head(halves[0], halves[1])
            rows = pl.ds(pl.multiple_of(h * PEER_TOPK, PEER_TOPK), PEER_TOPK)
            eid_scr[ci, rows, :] = eid
            gate_ref[ci, rows, :] = gate
        return carry
    lax.fori_loop(0, PEER_HEADS, head_body, 0)
    for ci in range(RT_T // PEER_TT):
        idx_ref[ci] = (eid_scr[ci].T * float(SUBLANES // 2)).astype(i32)


def route(xn2d, wqt_bf16, keys_bf16):
    n = xn2d.shape[0]
    n_tiles = n // PEER_TT
    per = RT_T // PEER_TT
    ospec = pl.BlockSpec((per, PEER_J, PEER_TT), lambda i: (i, 0, 0))
    return pl.pallas_call(
        _route_kernel,
        grid=(n // RT_T,),
        in_specs=[pl.BlockSpec((RT_T, D_MODEL), lambda i: (i, 0)),
                  pl.BlockSpec((2 * PEER_HEADS * PEER_HALF, D_MODEL), lambda i: (0, 0)),
                  pl.BlockSpec((2 * PEER_HEADS, PEER_N_KEYS, PEER_HALF), lambda i: (0, 0, 0))],
        out_specs=[pl.BlockSpec((per, PEER_TT, PEER_J), lambda i: (i, 0, 0)), ospec],
        out_shape=[jax.ShapeDtypeStruct((n_tiles, PEER_TT, PEER_J), i32),
                   jax.ShapeDtypeStruct((n_tiles, PEER_J, PEER_TT), f32)],
        scratch_shapes=[pltpu.VMEM((2 * PEER_HEADS * PEER_HALF, RT_T), f32),
                        pltpu.VMEM((per, PEER_J, PEER_TT), f32)],
        compiler_params=_cparams(("arbitrary",)),
        name="peer_route",
    )(xn2d, wqt_bf16, keys_bf16)


PEER_FLAT = PEER_TT * PEER_J
PACK_ROWS = SUBLANES // 2
PEER_NACC = 4


def _resident_tile_offset(step, n_steps, idx_hbm, idx_smem, sem):
    def copy(tile, slot):
        dst = idx_smem.at[pl.ds(pl.multiple_of(slot * PEER_FLAT, PEER_FLAT), PEER_FLAT)]
        return pltpu.make_async_copy(idx_hbm.at[tile], dst, sem.at[slot])

    slot = step % 2

    @pl.when(step == 0)
    def _():
        copy(0, 0).start()

    copy(step, slot).wait()

    @pl.when(step + 1 < n_steps)
    def _():
        copy(step + 1, 1 - slot).start()
    return slot * PEER_FLAT


def _sublane_fold(x, y, s, sub):
    clear = (sub & s) == 0
    a = jnp.where(clear, x, y)
    b = jnp.where(clear, y, x)
    if 2 * s == SUBLANES:
        partner = pltpu.roll(b, s, axis=0)
    else:
        partner = jnp.where(clear, pltpu.roll(b, SUBLANES - s, axis=0), pltpu.roll(b, s, axis=0))
    return a + partner


def _expert_row(tab_ref, row4):
    words = tab_ref[pl.ds(pl.multiple_of(row4, PACK_ROWS), PACK_ROWS), :]
    return pltpu.bitcast(words, bf16).astype(f32)


_BITREV3 = (0, 4, 2, 6, 1, 5, 3, 7)
PEER_GROUPS = PEER_J // SUBLANES


def _peer_u_kernel(idx_hbm, tab_ref, x_ref, gate_ref, w_ref, idx_smem, sem, h_scr):
    off = _resident_tile_offset(pl.program_id(0), pl.num_programs(0), idx_hbm, idx_smem, sem)
    lane = lax.broadcasted_iota(i32, (SUBLANES, PEER_TT), 1)
    sub = lax.broadcasted_iota(i32, (SUBLANES, LANES), 0)
    h_scr[...] = jnp.zeros((PEER_J, PEER_TT), f32)

    def retire(t_done, partial):
        for g in range(PEER_GROUPS):
            col = jnp.sum(partial[g], axis=-1, keepdims=True)
            rows = slice(g * SUBLANES, (g + 1) * SUBLANES)
            h_scr[rows, :] = jnp.where(lane == t_done, col, h_scr[rows, :])

    def tok_body(t, prev):
        retire(t - 1, prev)
        xt = x_ref[t]
        base = off + t * PEER_J
        partial = []
        for g in range(PEER_GROUPS):
            p = [_expert_row(tab_ref, idx_smem[base + g * SUBLANES + jj]) * xt for jj in _BITREV3]
            q = [_sublane_fold(p[2 * i], p[2 * i + 1], 4, sub) for i in range(4)]
            r = [_sublane_fold(q[2 * i], q[2 * i + 1], 2, sub) for i in range(2)]
            partial.append(_sublane_fold(r[0], r[1], 1, sub))
        return tuple(partial)
    zero = jnp.zeros((SUBLANES, LANES), f32)
    last = lax.fori_loop(0, PEER_TT, tok_body, (zero,) * PEER_GROUPS)
    retire(PEER_TT - 1, last)
    w_ref[...] = gate_ref[...] * jax.nn.gelu(h_scr[...])


def peer_u_pass(idx2, gate3, tab3, xn3):
    n_tiles = idx2.shape[0]
    tspec = pl.BlockSpec((None, PEER_J, PEER_TT), lambda i: (i, 0, 0))
    return pl.pallas_call(
        _peer_u_kernel,
        grid=(n_tiles,),
        in_specs=[pl.BlockSpec(memory_space=pl.ANY),
                  pl.BlockSpec(memory_space=pltpu.VMEM),
                  pl.BlockSpec((PEER_TT, SUBLANES, LANES), lambda i: (i, 0, 0)),
                  tspec],
        out_specs=tspec,
        out_shape=jax.ShapeDtypeStruct((n_tiles, PEER_J, PEER_TT), f32),
        scratch_shapes=[pltpu.SMEM((2 * PEER_FLAT,), i32),
                        pltpu.SemaphoreType.DMA((2,)),
                        pltpu.VMEM((PEER_J, PEER_TT), f32)],
        compiler_params=_cparams(("arbitrary",)),
        name="peer_u",
    )(idx2, tab3, xn3, gate3)


def _peer_v_kernel(idx_hbm, tab_ref, w_ref, h_ref, y_ref, idx_smem, sem):
    off = _resident_tile_offset(pl.program_id(0), pl.num_programs(0), idx_hbm, idx_smem, sem)
    lane = lax.broadcasted_iota(i32, (SUBLANES, PEER_TT), 1)

    def lane_splat(t):
        out = []
        for g in range(PEER_GROUPS):
            wg = w_ref[g * SUBLANES:(g + 1) * SUBLANES, :]
            col = jnp.sum(jnp.where(lane == t, wg, 0.0), axis=-1, keepdims=True)
            out.append(jnp.broadcast_to(col, (SUBLANES, LANES)))
        return tuple(out)

    def tok_body(t, wsplat):
        nxt = lane_splat(t + 1)
        base = off + t * PEER_J
        zero = jnp.zeros((SUBLANES, LANES), f32)
        accs = [h_ref[t]] + [zero] * (PEER_NACC - 1)
        for g in range(PEER_GROUPS):
            p = []
            for i in range(SUBLANES):
                row = _expert_row(tab_ref, idx_smem[base + g * SUBLANES + i])
                p.append(jnp.broadcast_to(wsplat[g][i:i + 1, :], (SUBLANES, LANES)) * row)
            s = ((p[0] + p[1]) + (p[2] + p[3])) + ((p[4] + p[5]) + (p[6] + p[7]))
            accs[g % PEER_NACC] = accs[g % PEER_NACC] + s
        y_ref[t] = (accs[0] + accs[1]) + (accs[2] + accs[3])
        return nxt
    lax.fori_loop(0, PEER_TT, tok_body, lane_splat(0))


def peer_v_pass(idx2, w3, tab3, h3):
    n_tiles = idx2.shape[0]
    xspec = pl.BlockSpec((PEER_TT, SUBLANES, LANES), lambda i: (i, 0, 0))
    return pl.pallas_call(
        _peer_v_kernel,
        grid=(n_tiles,),
        in_specs=[pl.BlockSpec(memory_space=pl.ANY),
                  pl.BlockSpec(memory_space=pltpu.VMEM),
                  pl.BlockSpec((None, PEER_J, PEER_TT), lambda i: (i, 0, 0)),
                  xspec],
        out_specs=xspec,
        out_shape=jax.ShapeDtypeStruct(h3.shape, f32),
        scratch_shapes=[pltpu.SMEM((2 * PEER_FLAT,), i32),
                        pltpu.SemaphoreType.DMA((2,))],
        compiler_params=_cparams(("arbitrary",)),
        name="peer_v",
    )(idx2, tab3, w3, h3)


def _trunk(x, p):
    b, seq, d = x.shape
    n = b * seq
    x2d = x.reshape(n, d)
    proj3 = in_proj(x2d, p["norm1"], p["w_in"]).reshape(b, seq, N_IN_COLS)
    att = attention(proj3, p["slopes"], p["q_gain"], p["k_gain"])
    hg = hgrn(proj3, p["lower_bounds"], p["hg_gain"])
    h1, xn2 = out_proj(x2d, att.reshape(n, ATT_WIDTH), hg.reshape(n, HG_WIDTH), p["w_out"], p["norm2"])
    idx3, gate3 = route(xn2, p["wq_t"], p["keys"])
    idx2 = idx3.reshape(-1, PEER_FLAT)
    w3 = peer_u_pass(idx2, gate3, p["u_tab"], xn2.reshape(n, SUBLANES, LANES))
    y3 = peer_v_pass(idx2, w3, p["v_tab"], h1.reshape(n, SUBLANES, LANES))
    return y3.reshape(b, seq, d)


def _pack_table(tab):
    n_exp = tab.shape[0]
    bits = lax.bitcast_convert_type(tab.astype(bf16), jnp.uint16).reshape(n_exp, SUBLANES // 2, 2, LANES)
    low = bits[:, :, 0, :].astype(jnp.uint32)
    high = bits[:, :, 1, :].astype(jnp.uint32)
    return (low | (high << 16)).reshape(n_exp * (SUBLANES // 2), LANES)


def kernel(x_prompt, x_sample, norm1, w_in, q_norm, k_norm, lower_bounds, hg_norm, w_out, norm2,
           peer_wq, peer_keys, peer_u, peer_v):
    n_heads = ATT_WIDTH // HEAD_DIM
    p = {
        "norm1": norm1[0].reshape(1, D_MODEL),
        "w_in": w_in[0].astype(bf16),
        "q_gain": jnp.tile(q_norm[0], 2).reshape(1, LANES),
        "k_gain": jnp.tile(k_norm[0], 2).reshape(1, LANES),
        "hg_gain": jnp.tile(hg_norm[0], 2).reshape(1, LANES),
        "slopes": 2.0 ** (-8.0 * jnp.arange(1, n_heads + 1, dtype=f32) / n_heads),
        "lower_bounds": lower_bounds.astype(f32),
        "w_out": w_out[0].astype(bf16),
        "norm2": norm2[0].reshape(1, D_MODEL),
        "wq_t": peer_wq[0].T.astype(bf16),
        "keys": peer_keys[0].reshape(2 * PEER_HEADS, PEER_N_KEYS, PEER_HALF).astype(bf16),
        "u_tab": _pack_table(peer_u[0]),
        "v_tab": _pack_table(peer_v[0]),
    }
    return (_trunk(x_prompt, p), _trunk(x_sample, p))
```

```python
import functools

import jax
import jax.numpy as jnp
from jax import lax
from jax.experimental import pallas as pl
from jax.experimental.pallas import tpu as pltpu

f32 = jnp.float32
bf16 = jnp.bfloat16
i32 = jnp.int32

D_MODEL = 1024
ATT_WIDTH = 512
HG_WIDTH = 512
HEAD_DIM = 64
N_IN_COLS = 3 * ATT_WIDTH + 5 * HG_WIDTH
DILATIONS = (16, 4, 1)
KEYS_PER_SIDE = 64
HG_CHUNK = 64
PEER_HEADS = 8
PEER_N_KEYS = 128
PEER_TOPK = 16
PEER_HALF = 128
NORM_EPS = 1e-6

LANES = 128
SUBLANES = 8
VMEM_LIMIT = 56 * 1024 * 1024
NEG_INF = float("-inf")


DOUBLE_BUFFER_MAX_SEQ = 4096


def _column_buffering(seq):
    return pl.Buffered(2) if seq <= DOUBLE_BUFFER_MAX_SEQ else pl.Buffered(1)


def _cparams(sem=None, flags=None):
    return pltpu.CompilerParams(dimension_semantics=sem, vmem_limit_bytes=VMEM_LIMIT, flags=flags)


IN_TM = 512


def _inproj_kernel(x_ref, g_ref, w_ref, o_ref):
    x = x_ref[...]
    ms = jnp.mean(x * x, axis=-1, keepdims=True)
    xn = ((x * lax.rsqrt(ms + NORM_EPS)) * g_ref[...]).astype(bf16)
    tn = 1024
    for c in range(N_IN_COLS // tn):
        o_ref[:, c * tn:(c + 1) * tn] = jnp.dot(xn, w_ref[:, c * tn:(c + 1) * tn],
                                                preferred_element_type=f32)


def in_proj(x2d, gain, w_bf16):
    n = x2d.shape[0]
    return pl.pallas_call(
        _inproj_kernel,
        grid=(n // IN_TM,),
        in_specs=[pl.BlockSpec((IN_TM, D_MODEL), lambda i: (i, 0)),
                  pl.BlockSpec((1, D_MODEL), lambda i: (0, 0)),
                  pl.BlockSpec((D_MODEL, N_IN_COLS), lambda i: (0, 0))],
        out_specs=pl.BlockSpec((IN_TM, N_IN_COLS), lambda i: (i, 0)),
        out_shape=jax.ShapeDtypeStruct((n, N_IN_COLS), f32),
        compiler_params=_cparams(("arbitrary",)),
        name="in_proj",
    )(x2d, gain, w_bf16)


def _pair_rmsnorm(x, gain, same_head):
    x2 = x * x
    hi = x2.astype(bf16)
    lo = (x2 - hi.astype(f32)).astype(bf16)
    ss = (jnp.dot(hi, same_head, preferred_element_type=f32)
          + jnp.dot(lo, same_head, preferred_element_type=f32))
    return (x * lax.rsqrt(ss * (1.0 / HEAD_DIM) + NORM_EPS)) * gain


def _same_head_matrix():
    r = lax.broadcasted_iota(i32, (LANES, LANES), 0) < HEAD_DIM
    c = lax.broadcasted_iota(i32, (LANES, LANES), 1) < HEAD_DIM
    return (r == c).astype(bf16)


ATT_QB = 128
ATT_KW = ATT_QB + 2 * KEYS_PER_SIDE
ATT_CP = 256
ATT_CPU = 4
ATT_NB = 8


def _attn_kernel(slopes_ref, q_ref, k_ref, v_ref, qg_ref, kg_ref, o_ref,
                 qc, kc, vc, oacc, macc, lacc, *, seq):
    hp = pl.program_id(1)
    lane = lax.broadcasted_iota(i32, (1, LANES), 1)
    head0 = lane < HEAD_DIM
    same_head = _same_head_matrix()
    qg = qg_ref[...]
    kg = kg_ref[...]
    zpad = jnp.zeros((KEYS_PER_SIDE, LANES), bf16)
    for ref in (qc, kc, vc):
        ref[0:KEYS_PER_SIDE, :] = zpad
        ref[seq + KEYS_PER_SIDE:seq + 2 * KEYS_PER_SIDE, :] = zpad
    slope = (slopes_ref[2 * hp], slopes_ref[2 * hp + 1])

    iq = lax.broadcasted_iota(i32, (ATT_QB, ATT_KW), 0)
    jj = lax.broadcasted_iota(i32, (ATT_QB, ATT_KW), 1)
    rel = jnp.abs(iq + KEYS_PER_SIDE - jj)
    band = rel <= KEYS_PER_SIDE

    for pat, dil in enumerate(DILATIONS):
        cls_len = seq // dil
        shift = cls_len.bit_length() - 1
        cp = min(ATT_CP, cls_len)
        n_cp = cls_len // cp

        def copy_body(i, carry, dil=dil, cp=cp, n_cp=n_cp):
            for u in range(ATT_CPU):
                chunk = ATT_CPU * i + u
                if dil == 1:
                    src = pl.ds(pl.multiple_of(chunk * cp, cp), cp)
                else:
                    r = lax.shift_right_logical(chunk, n_cp.bit_length() - 1)
                    c = chunk & (n_cp - 1)
                    src = pl.ds(r + dil * cp * c, cp, stride=dil)
                dst = pl.ds(pl.multiple_of(KEYS_PER_SIDE + chunk * cp, KEYS_PER_SIDE), cp)
                qn = _pair_rmsnorm(q_ref[src, :], qg, same_head) * (HEAD_DIM ** -0.5)
                qc[dst, :] = qn.astype(bf16)
                kc[dst, :] = _pair_rmsnorm(k_ref[src, :], kg, same_head).astype(bf16)
                vc[dst, :] = v_ref[src, :].astype(bf16)
            return carry
        lax.fori_loop(0, seq // (cp * ATT_CPU), copy_body, 0)

        relf = (dil * rel).astype(f32)
        bias = [jnp.where(band, -slope[a] * relf, NEG_INF) for a in range(2)]

        def one_block(i, dil=dil, cls_len=cls_len, shift=shift, bias=bias):
            r0 = pl.multiple_of(i * ATT_QB, ATT_QB)
            cls = lax.shift_right_logical(r0, shift)
            lo = cls * cls_len - (r0 - KEYS_PER_SIDE)
            valid = (jj >= lo) & (jj < lo + cls_len)
            qb = qc[pl.ds(pl.multiple_of(r0 + KEYS_PER_SIDE, KEYS_PER_SIDE), ATT_QB), :]
            kw = kc[pl.ds(r0, ATT_KW), :]
            vw = vc[pl.ds(r0, ATT_KW), :]
            res = []
            for a in range(2):
                hm = head0 if a == 0 else jnp.logical_not(head0)
                qa = jnp.where(hm, qb, jnp.zeros_like(qb))
                s = lax.dot_general(qa, kw, (((1,), (1,)), ((), ())), preferred_element_type=f32)
                s = jnp.where(valid, s + bias[a], NEG_INF)
                m = jnp.max(s, axis=-1, keepdims=True)
                p = jnp.exp(s - m)
                l = jnp.sum(p, axis=-1, keepdims=True)
                o = jnp.dot(p.astype(bf16), vw, preferred_element_type=f32)
                res.append((o, m, l))
            o = jnp.where(head0, res[0][0], res[1][0])
            m = jnp.where(head0, res[0][1], res[1][1])
            l = jnp.where(head0, res[0][2], res[1][2])
            if dil == 1:
                tok = pl.ds(r0, ATT_QB)
            else:
                tok = pl.ds((r0 - cls * cls_len) * dil + cls, ATT_QB, stride=dil)
            return tok, o, m, l

        def blocks_body(i, carry, pat=pat, one_block=one_block):
            done = [one_block(ATT_NB * i + u) for u in range(ATT_NB)]
            if pat == 0:
                for tok, o, m, l in done:
                    oacc[tok, :] = o
                    macc[tok, :] = m
                    lacc[tok, :] = l
            else:
                old = [(macc[tok, :], oacc[tok, :], lacc[tok, :]) for tok, _, _, _ in done]
                for (tok, o, m, l), (m_old, o_old, l_old) in zip(done, old):
                    m_new = jnp.maximum(m_old, m)
                    a_old = jnp.exp(m_old - m_new)
                    a_new = jnp.exp(m - m_new)
                    oacc[tok, :] = a_old * o_old + a_new * o
                    lacc[tok, :] = a_old * l_old + a_new * l
                    macc[tok, :] = m_new
            return carry
        lax.fori_loop(0, seq // (ATT_QB * ATT_NB), blocks_body, 0)

    def fin_body(c, carry):
        rows = pl.ds(pl.multiple_of(c * ATT_CP, ATT_CP), ATT_CP)
        o_ref[rows, :] = (oacc[rows, :] / lacc[rows, :]).astype(bf16)
        return carry
    lax.fori_loop(0, seq // ATT_CP, fin_body, 0)


def attention(proj3, slopes, qg2, kg2):
    b, seq, _ = proj3.shape
    n_hp = ATT_WIDTH // LANES

    def col(off):
        return pl.BlockSpec((None, seq, LANES), lambda bi, h, off=off: (bi, 0, off + h),
                            pipeline_mode=_column_buffering(seq))
    gspec = pl.BlockSpec((1, LANES), lambda bi, h: (0, 0))
    return pl.pallas_call(
        functools.partial(_attn_kernel, seq=seq),
        grid=(b, n_hp),
        in_specs=[pl.BlockSpec(memory_space=pltpu.SMEM),
                  col(0), col(n_hp), col(2 * n_hp), gspec, gspec],
        out_specs=pl.BlockSpec((None, seq, LANES), lambda bi, h: (bi, 0, h)),
        out_shape=jax.ShapeDtypeStruct((b, seq, ATT_WIDTH), bf16),
        scratch_shapes=[pltpu.VMEM((seq + 2 * KEYS_PER_SIDE, LANES), bf16)] * 3
                       + [pltpu.VMEM((seq, LANES), f32)] * 3,
        compiler_params=_cparams(("arbitrary", "arbitrary")),
        name="dilated_attention",
    )(slopes, proj3, proj3, proj3, qg2, kg2)


HG_LEVELS = 6
HG_PER_ITER = 8


def _shift_rows(x, j, up):
    n = x.shape[0]
    return pltpu.roll(x, (n - j) % n if up else j, axis=0)


def _hg_chunk(q, k, v, lf, rev, head0_rows, blockdiag):
    c = HG_CHUNK
    row = lax.broadcasted_iota(i32, (c, LANES), 0)
    cum = lf
    for lv in range(HG_LEVELS):
        j = 1 << lv
        sh = _shift_rows(cum, j, up=rev)
        ok = (row < c - j) if rev else (row >= j)
        cum = cum + jnp.where(ok, sh, 0.0)

    trow = lax.broadcasted_iota(i32, (2 * c, c), 0) & (c - 1)
    scol = lax.broadcasted_iota(i32, (2 * c, c), 1)
    differ = trow ^ scol
    q_h0 = jnp.where(head0_rows, q, 0.0)
    q_h1 = q - q_h0

    def pair_scores(a_decay, b_mat):
        a2 = jnp.concatenate([q_h0 * a_decay, q_h1 * a_decay], axis=0) if a_decay is not None else \
            jnp.concatenate([q_h0, q_h1], axis=0)
        return lax.dot_general(a2.astype(bf16), b_mat.astype(bf16), (((1,), (1,)), ((), ())),
                               preferred_element_type=f32)

    scores = pair_scores(None, k)
    edge = cum
    for lv in range(1, HG_LEVELS + 1):
        h = 1 << (lv - 1)
        if h > 1:
            hh = h >> 1
            sh = _shift_rows(edge, hh, up=not rev)
            take = ((row & hh) == 0) != rev
            edge = jnp.where(take, sh, edge)
        far = ((row & h) != 0) != rev
        edge_prev = _shift_rows(edge, h, up=rev)
        a_decay = jnp.exp(jnp.where(far, cum - edge_prev, NEG_INF))
        b_mat = k * jnp.exp(jnp.where(far, NEG_INF, edge - cum))
        scores = jnp.where(differ < h, scores, pair_scores(a_decay, b_mat))

    hh = c >> 1
    sh = _shift_rows(edge, hh, up=not rev)
    total = jnp.where(((row & hh) == 0) != rev, sh, edge)

    vb = v.astype(bf16)
    scb = scores.astype(bf16)
    intra0 = jnp.dot(scb[0:c], vb, preferred_element_type=f32)
    intra1 = jnp.dot(scb[c:2 * c], vb, preferred_element_type=f32)
    intra = jnp.where(head0_rows, intra0, intra1)
    qe = (q * jnp.exp(cum)).astype(bf16)
    kd = (k * jnp.exp(total - cum)).astype(bf16)
    upd = lax.dot_general(vb, kd, (((0,), (0,)), ((), ())), preferred_element_type=f32)
    return intra, qe, jnp.where(blockdiag, upd, 0.0), jnp.exp(total[0:1, :])


def _hg_state_step(st, intra, qe, upd, decay):
    inter = lax.dot_general(qe, st.astype(bf16), (((1,), (1,)), ((), ())), preferred_element_type=f32)
    return inter + intra, st * decay + upd


def _hgrn_kernel(lb_ref, q_ref, zf_ref, zb_ref, i_ref, g_ref, hn_ref, o_ref, of_scr, ob_scr, *, seq):
    c = HG_CHUNK
    head0_rows = lax.broadcasted_iota(i32, (c, LANES), 1) < HEAD_DIM
    r2 = lax.broadcasted_iota(i32, (LANES, LANES), 0) < HEAD_DIM
    c2 = lax.broadcasted_iota(i32, (LANES, LANES), 1) < HEAD_DIM
    blockdiag = r2 == c2

    lbs = []
    for d in range(2):
        a0 = lb_ref[d, 0:1, :]
        a1 = lb_ref[d, 1:2, :]
        mx = jnp.maximum(a0, a1)
        e0 = jnp.exp(a0 - mx)
        e1 = jnp.exp(a1 - mx)
        lbs.append(e0 / (e0 + e1))

    n_chunks = seq // c

    def gates(z, lb):
        f = lb + (1.0 - lb) * jax.nn.sigmoid(z)
        return jnp.log(f), 1.0 - f

    def scan_body(it, states):
        st_f, st_b = states
        work = []
        for u in range(HG_PER_ITER):
            ci = it * HG_PER_ITER + u
            rows_f = pl.ds(pl.multiple_of(ci * c, c), c)
            rows_b = pl.ds(pl.multiple_of((n_chunks - 1 - ci) * c, c), c)
            lf_f, k_f = gates(zf_ref[rows_f, :], lbs[0])
            lf_b, k_b = gates(zb_ref[rows_b, :], lbs[1])
            work.append((rows_f, _hg_chunk(q_ref[rows_f, :], k_f, i_ref[rows_f, :], lf_f, False,
                                           head0_rows, blockdiag),
                         rows_b, _hg_chunk(q_ref[rows_b, :], k_b, i_ref[rows_b, :], lf_b, True,
                                           head0_rows, blockdiag)))
        for rows_f, part_f, rows_b, part_b in work:
            o_f, st_f = _hg_state_step(st_f, *part_f)
            o_b, st_b = _hg_state_step(st_b, *part_b)
            of_scr[rows_f, :] = o_f
            ob_scr[rows_b, :] = o_b
        return st_f, st_b
    zero_state = jnp.zeros((LANES, LANES), f32)
    lax.fori_loop(0, n_chunks // HG_PER_ITER, scan_body, (zero_state, zero_state))

    hn = hn_ref[...]
    same_head = blockdiag.astype(bf16)
    fin = 4 * c

    def fin_body(ci, carry):
        rows = pl.ds(pl.multiple_of(ci * fin, fin), fin)
        y = _pair_rmsnorm(of_scr[rows, :] + ob_scr[rows, :], hn, same_head)
        o_ref[rows, :] = (y * jax.nn.silu(g_ref[rows, :])).astype(bf16)
        return carry
    lax.fori_loop(0, seq // fin, fin_body, 0)


def hgrn(proj3, lower_bounds, hn2):
    b, seq, _ = proj3.shape
    n_hp = HG_WIDTH // LANES
    base = 3 * ATT_WIDTH // LANES

    def col(k):
        return pl.BlockSpec((None, seq, LANES), lambda bi, h, k=k: (bi, 0, base + k * n_hp + h),
                            pipeline_mode=_column_buffering(seq))
    return pl.pallas_call(
        functools.partial(_hgrn_kernel, seq=seq),
        grid=(b, n_hp),
        in_specs=[pl.BlockSpec((2, 2, LANES), lambda bi, h: (0, 0, h)),
                  col(0), col(1), col(2), col(3), col(4),
                  pl.BlockSpec((1, LANES), lambda bi, h: (0, 0))],
        out_specs=pl.BlockSpec((None, seq, LANES), lambda bi, h: (bi, 0, h)),
        out_shape=jax.ShapeDtypeStruct((b, seq, HG_WIDTH), bf16),
        scratch_shapes=[pltpu.VMEM((seq, LANES), f32)] * 2,
        compiler_params=_cparams(("arbitrary", "arbitrary")),
        name="hgrn2_bidir",
    )(lower_bounds, proj3, proj3, proj3, proj3, proj3, hn2)


OUT_TM = 512


def _outproj_kernel(x_ref, att_ref, hg_ref, w_ref, g_ref, h_ref, xn_ref):
    h = (x_ref[...]
         + jnp.dot(att_ref[...], w_ref[0:ATT_WIDTH, :], preferred_element_type=f32)
         + jnp.dot(hg_ref[...], w_ref[ATT_WIDTH:, :], preferred_element_type=f32))
    h_ref[...] = h
    ms = jnp.mean(h * h, axis=-1, keepdims=True)
    xn_ref[...] = (h * lax.rsqrt(ms + NORM_EPS)) * g_ref[...]


def out_proj(x2d, att2d, hg2d, w_bf16, gain):
    n = x2d.shape[0]
    row = lambda w: pl.BlockSpec((OUT_TM, w), lambda i: (i, 0))
    return pl.pallas_call(
        _outproj_kernel,
        grid=(n // OUT_TM,),
        in_specs=[row(D_MODEL), row(ATT_WIDTH), row(HG_WIDTH),
                  pl.BlockSpec((D_MODEL, D_MODEL), lambda i: (0, 0)),
                  pl.BlockSpec((1, D_MODEL), lambda i: (0, 0))],
        out_specs=[row(D_MODEL), row(D_MODEL)],
        out_shape=[jax.ShapeDtypeStruct((n, D_MODEL), f32)] * 2,
        compiler_params=_cparams(("arbitrary",)),
        name="out_proj",
    )(x2d, att2d, hg2d, w_bf16, gain)


RT_T = 1024
PEER_TT = 128
PEER_J = PEER_HEADS * PEER_TOPK
BIG = 1.0e9


def _top16(s, posf):
    vals, poss = [], []
    for _ in range(PEER_TOPK):
        m = jnp.max(s, axis=0, keepdims=True)
        pos = jnp.min(jnp.where(s == m, posf, BIG), axis=0, keepdims=True)
        vals.append(m)
        poss.append(pos)
        s = jnp.where(posf == pos, NEG_INF, s)
    return vals, poss


def _stack16(rows):
    r16 = lax.broadcasted_iota(i32, (PEER_TOPK, rows[0].shape[1]), 0)
    out = jnp.zeros((PEER_TOPK, rows[0].shape[1]), f32)
    for kk, rw in enumerate(rows):
        out = jnp.where(r16 == kk, rw, out)
    return out


def _route_head(s0, s1):
    cdim = s0.shape[1]
    keypos = lax.broadcasted_iota(i32, (PEER_N_KEYS, cdim), 0).astype(f32)
    v0, p0 = _top16(s0, keypos)
    v1, p1 = _top16(s1, keypos)
    sv0, sp0, sv1, sp1 = _stack16(v0), _stack16(p0), _stack16(v1), _stack16(p1)

    r8 = lax.broadcasted_iota(i32, (SUBLANES, cdim), 0)
    groups_v = [v0[0] + sv1]
    groups_e = [p0[0] * PEER_N_KEYS + sp1]
    groups_p = [lax.broadcasted_iota(i32, (PEER_TOPK, cdim), 0).astype(f32)]
    for i in range(1, 8):
        nj = PEER_TOPK // (i + 1)
        ok = r8 < nj
        groups_v.append(jnp.where(ok, v0[i] + sv1[0:SUBLANES], NEG_INF))
        groups_e.append(p0[i] * PEER_N_KEYS + sp1[0:SUBLANES])
        groups_p.append((r8 + i * PEER_TOPK).astype(f32))
    groups_v.append(sv0[SUBLANES:] + v1[0])
    groups_e.append(sp0[SUBLANES:] * PEER_N_KEYS + p1[0])
    groups_p.append(((r8 + SUBLANES) * PEER_TOPK).astype(f32))
    cand = jnp.concatenate(groups_v, axis=0)
    cexp = jnp.concatenate(groups_e, axis=0)
    cpos = jnp.concatenate(groups_p, axis=0)

    vals, exps = [], []
    for _ in range(PEER_TOPK):
        m = jnp.max(cand, axis=0, keepdims=True)
        pos = jnp.min(jnp.where(cand == m, cpos, BIG), axis=0, keepdims=True)
        sel = cpos == pos
        exps.append(jnp.max(jnp.where(sel, cexp, -1.0), axis=0, keepdims=True))
        vals.append(m)
        cand = jnp.where(sel, NEG_INF, cand)
    sv = _stack16(vals)
    ex = jnp.exp(sv - vals[0])
    gate = ex / jnp.sum(ex, axis=0, keepdims=True)
    return _stack16(exps), gate


def _route_kernel(xn_ref, wqt_ref, keys_ref, idx_ref, gate_ref, qt_scr, eid_scr):
    xb = xn_ref[...].astype(bf16)
    qt_scr[...] = lax.dot_general(wqt_ref[...], xb, (((1,), (1,)), ((), ())),
                                  preferred_element_type=f32)

    def head_body(h, carry):
        for ci in range(RT_T // PEER_TT):
            cols = slice(ci * PEER_TT, (ci + 1) * PEER_TT)
            halves = []
            for p in range(2):
                r0 = pl.multiple_of((2 * h + p) * PEER_HALF, PEER_HALF)
                qh = qt_scr[pl.ds(r0, PEER_HALF), cols].astype(bf16)
                halves.append(jnp.dot(keys_ref[2 * h + p], qh, preferred_element_type=f32))
            eid, gate = _route_head(halves[0], halves[1])
            rows = pl.ds(pl.multiple_of(h * PEER_TOPK, PEER_TOPK), PEER_TOPK)
            eid_scr[ci, rows, :] = eid
            gate_ref[ci, rows, :] = gate
        return carry
    lax.fori_loop(0, PEER_HEADS, head_body, 0)
    for ci in range(RT_T // PEER_TT):
        idx_ref[ci] = (eid_scr[ci].T * float(SUBLANES // 2)).astype(i32)


def route(xn2d, wqt_bf16, keys_bf16):
    n = xn2d.shape[0]
    n_tiles = n // PEER_TT
    per = RT_T // PEER_TT
    ospec = pl.BlockSpec((per, PEER_J, PEER_TT), lambda i: (i, 0, 0))
    return pl.pallas_call(
        _route_kernel,
        grid=(n // RT_T,),
        in_specs=[pl.BlockSpec((RT_T, D_MODEL), lambda i: (i, 0)),
                  pl.BlockSpec((2 * PEER_HEADS * PEER_HALF, D_MODEL), lambda i: (0, 0)),
                  pl.BlockSpec((2 * PEER_HEADS, PEER_N_KEYS, PEER_HALF), lambda i: (0, 0, 0))],
        out_specs=[pl.BlockSpec((per, PEER_TT, PEER_J), lambda i: (i, 0, 0)), ospec],
        out_shape=[jax.ShapeDtypeStruct((n_tiles, PEER_TT, PEER_J), i32),
                   jax.ShapeDtypeStruct((n_tiles, PEER_J, PEER_TT), f32)],
        scratch_shapes=[pltpu.VMEM((2 * PEER_HEADS * PEER_HALF, RT_T), f32),
                        pltpu.VMEM((per, PEER_J, PEER_TT), f32)],
        compiler_params=_cparams(("arbitrary",)),
        name="peer_route",
    )(xn2d, wqt_bf16, keys_bf16)


PEER_FLAT = PEER_TT * PEER_J
PACK_ROWS = SUBLANES // 2
PEER_NACC = 4


def _resident_tile_offset(step, n_steps, idx_hbm, idx_smem, sem):
    def copy(tile, slot):
        dst = idx_smem.at[pl.ds(pl.multiple_of(slot * PEER_FLAT, PEER_FLAT), PEER_FLAT)]
        return pltpu.make_async_copy(idx_hbm.at[tile], dst, sem.at[slot])

    slot = step % 2

    @pl.when(step == 0)
    def _():
        copy(0, 0).start()

    copy(step, slot).wait()

    @pl.when(step + 1 < n_steps)
    def _():
        copy(step + 1, 1 - slot).start()
    return slot * PEER_FLAT


def _sublane_fold(x, y, s, sub):
    clear = (sub & s) == 0
    a = jnp.where(clear, x, y)
    b = jnp.where(clear, y, x)
    if 2 * s == SUBLANES:
        partner = pltpu.roll(b, s, axis=0)
    else:
        partner = jnp.where(clear, pltpu.roll(b, SUBLANES - s, axis=0), pltpu.roll(b, s, axis=0))
    return a + partner


def _expert_row(tab_ref, row4):
    words = tab_ref[pl.ds(pl.multiple_of(row4, PACK_ROWS), PACK_ROWS), :]
    return pltpu.bitcast(words, bf16).astype(f32)


_BITREV3 = (0, 4, 2, 6, 1, 5, 3, 7)
PEER_GROUPS = PEER_J // SUBLANES


def _peer_u_kernel(idx_hbm, tab_ref, x_ref, gate_ref, w_ref, idx_smem, sem, h_scr):
    off = _resident_tile_offset(pl.program_id(0), pl.num_programs(0), idx_hbm, idx_smem, sem)
    lane = lax.broadcasted_iota(i32, (SUBLANES, PEER_TT), 1)
    sub = lax.broadcasted_iota(i32, (SUBLANES, LANES), 0)
    h_scr[...] = jnp.zeros((PEER_J, PEER_TT), f32)

    def retire(t_done, partial):
        for g in range(PEER_GROUPS):
            col = jnp.sum(partial[g], axis=-1, keepdims=True)
            rows = slice(g * SUBLANES, (g + 1) * SUBLANES)
            h_scr[rows, :] = jnp.where(lane == t_done, col, h_scr[rows, :])

    def tok_body(t, prev):
        retire(t - 1, prev)
        xt = x_ref[t]
        base = off + t * PEER_J
        partial = []
        for g in range(PEER_GROUPS):
            p = [_expert_row(tab_ref, idx_smem[base + g * SUBLANES + jj]) * xt for jj in _BITREV3]
            q = [_sublane_fold(p[2 * i], p[2 * i + 1], 4, sub) for i in range(4)]
            r = [_sublane_fold(q[2 * i], q[2 * i + 1], 2, sub) for i in range(2)]
            partial.append(_sublane_fold(r[0], r[1], 1, sub))
        return tuple(partial)
    zero = jnp.zeros((SUBLANES, LANES), f32)
    last = lax.fori_loop(0, PEER_TT, tok_body, (zero,) * PEER_GROUPS)
    retire(PEER_TT - 1, last)
    w_ref[...] = gate_ref[...] * jax.nn.gelu(h_scr[...])


def peer_u_pass(idx2, gate3, tab3, xn3):
    n_tiles = idx2.shape[0]
    tspec = pl.BlockSpec((None, PEER_J, PEER_TT), lambda i: (i, 0, 0))
    return pl.pallas_call(
        _peer_u_kernel,
        grid=(n_tiles,),
        in_specs=[pl.BlockSpec(memory_space=pl.ANY),
                  pl.BlockSpec(memory_space=pltpu.VMEM),
                  pl.BlockSpec((PEER_TT, SUBLANES, LANES), lambda i: (i, 0, 0)),
                  tspec],
        out_specs=tspec,
        out_shape=jax.ShapeDtypeStruct((n_tiles, PEER_J, PEER_TT), f32),
        scratch_shapes=[pltpu.SMEM((2 * PEER_FLAT,), i32),
                        pltpu.SemaphoreType.DMA((2,)),
                        pltpu.VMEM((PEER_J, PEER_TT), f32)],
        compiler_params=_cparams(("arbitrary",)),
        name="peer_u",
    )(idx2, tab3, xn3, gate3)


def _peer_v_kernel(idx_hbm, tab_ref, w_ref, h_ref, y_ref, idx_smem, sem):
    off = _resident_tile_offset(pl.program_id(0), pl.num_programs(0), idx_hbm, idx_smem, sem)
    lane = lax.broadcasted_iota(i32, (SUBLANES, PEER_TT), 1)

    def lane_splat(t):
        out = []
        for g in range(PEER_GROUPS):
            wg = w_ref[g * SUBLANES:(g + 1) * SUBLANES, :]
            col = jnp.sum(jnp.where(lane == t, wg, 0.0), axis=-1, keepdims=True)
            out.append(jnp.broadcast_to(col, (SUBLANES, LANES)))
        return tuple(out)

    def tok_body(t, wsplat):
        nxt = lane_splat(t + 1)
        base = off + t * PEER_J
        zero = jnp.zeros((SUBLANES, LANES), f32)
        accs = [h_ref[t]] + [zero] * (PEER_NACC - 1)
        for g in range(PEER_GROUPS):
            p = []
            for i in range(SUBLANES):
                row = _expert_row(tab_ref, idx_smem[base + g * SUBLANES + i])
                p.append(jnp.broadcast_to(wsplat[g][i:i + 1, :], (SUBLANES, LANES)) * row)
            s = ((p[0] + p[1]) + (p[2] + p[3])) + ((p[4] + p[5]) + (p[6] + p[7]))
            accs[g % PEER_NACC] = accs[g % PEER_NACC] + s
        y_ref[t] = (accs[0] + accs[1]) + (accs[2] + accs[3])
        return nxt
    lax.fori_loop(0, PEER_TT, tok_body, lane_splat(0))


def peer_v_pass(idx2, w3, tab3, h3):
    n_tiles = idx2.shape[0]
    xspec = pl.BlockSpec((PEER_TT, SUBLANES, LANES), lambda i: (i, 0, 0))
    return pl.pallas_call(
        _peer_v_kernel,
        grid=(n_tiles,),
        in_specs=[pl.BlockSpec(memory_space=pl.ANY),
                  pl.BlockSpec(memory_space=pltpu.VMEM),
                  pl.BlockSpec((None, PEER_J, PEER_TT), lambda i: (i, 0, 0)),
                  xspec],
        out_specs=xspec,
        out_shape=jax.ShapeDtypeStruct(h3.shape, f32),
        scratch_shapes=[pltpu.SMEM((2 * PEER_FLAT,), i32),
                        pltpu.SemaphoreType.DMA((2,))],
        compiler_params=_cparams(("arbitrary",)),
        name="peer_v",
    )(idx2, tab3, w3, h3)


def _trunk(x, p):
    b, seq, d = x.shape
    n = b * seq
    x2d = x.reshape(n, d)
    proj3 = in_proj(x2d, p["norm1"], p["w_in"]).reshape(b, seq, N_IN_COLS)
    att = attention(proj3, p["slopes"], p["q_gain"], p["k_gain"])
    hg = hgrn(proj3, p["lower_bounds"], p["hg_gain"])
    h1, xn2 = out_proj(x2d, att.reshape(n, ATT_WIDTH), hg.reshape(n, HG_WIDTH), p["w_out"], p["norm2"])
    idx3, gate3 = route(xn2, p["wq_t"], p["keys"])
    idx2 = idx3.reshape(-1, PEER_FLAT)
    w3 = peer_u_pass(idx2, gate3, p["u_tab"], xn2.reshape(n, SUBLANES, LANES))
    y3 = peer_v_pass(idx2, w3, p["v_tab"], h1.reshape(n, SUBLANES, LANES))
    return y3.reshape(b, seq, d)


def _pack_table(tab):
    n_exp = tab.shape[0]
    bits = lax.bitcast_convert_type(tab.astype(bf16), jnp.uint16).reshape(n_exp, SUBLANES // 2, 2, LANES)
    low = bits[:, :, 0, :].astype(jnp.uint32)
    high = bits[:, :, 1, :].astype(jnp.uint32)
    return (low | (high << 16)).reshape(n_exp * (SUBLANES // 2), LANES)


def kernel(x_prompt, x_sample, norm1, w_in, q_norm, k_norm, lower_bounds, hg_norm, w_out, norm2,
           peer_wq, peer_keys, peer_u, peer_v):
    n_heads = ATT_WIDTH // HEAD_DIM
    p = {
        "norm1": norm1[0].reshape(1, D_MODEL),
        "w_in": w_in[0].astype(bf16),
        "q_gain": jnp.tile(q_norm[0], 2).reshape(1, LANES),
        "k_gain": jnp.tile(k_norm[0], 2).reshape(1, LANES),
        "hg_gain": jnp.tile(hg_norm[0], 2).reshape(1, LANES),
        "slopes": 2.0 ** (-8.0 * jnp.arange(1, n_heads + 1, dtype=f32) / n_heads),
        "lower_bounds": lower_bounds.astype(f32),
        "w_out": w_out[0].astype(bf16),
        "norm2": norm2[0].reshape(1, D_MODEL),
        "wq_t": peer_wq[0].T.astype(bf16),
        "keys": peer_keys[0].reshape(2 * PEER_HEADS, PEER_N_KEYS, PEER_HALF).astype(bf16),
        "u_tab": _pack_table(peer_u[0]),
        "v_tab": _pack_table(peer_v[0]),
    }
    return (_trunk(x_prompt, p), _trunk(x_sample, p))
```
